```python
import jax, jax.numpy as jnp
from jax import lax
import numpy as np

D_MODEL = 2048
BATCH = 4
SEQ = 4096
DEPTH = 2

HEAD_DIM = 128
MLSTM_WIDTH = D_MODEL // 4
MLSTM_HEADS = MLSTM_WIDTH // HEAD_DIM
POOL_WIDTH = D_MODEL // 4
POOL_WINDOWS = (2, 4, 8, 16)
POOL_GROUPS = len(POOL_WINDOWS)
POOL_GROUP_DIM = POOL_WIDTH // POOL_GROUPS
FOX_WIDTH = D_MODEL - MLSTM_WIDTH - POOL_WIDTH
FOX_HEADS = FOX_WIDTH // HEAD_DIM
D_FF = ((8 * D_MODEL // 3 + 255) // 256) * 256
CONV_WIDTH = 4
MLSTM_CHUNK = 128
FOX_BLOCK = 128
RMS_EPS = 1e-6
FFN_RESIDUAL_WEIGHT = 0.5

OFF_MQ = 0
OFF_MK = OFF_MQ + MLSTM_WIDTH
OFF_MV = OFF_MK + MLSTM_WIDTH
OFF_MO = OFF_MV + MLSTM_WIDTH
OFF_MI = OFF_MO + MLSTM_WIDTH
OFF_MF = OFF_MI + MLSTM_HEADS
OFF_POOL = OFF_MF + MLSTM_HEADS
OFF_AQ = OFF_POOL + POOL_WIDTH
OFF_AK = OFF_AQ + FOX_WIDTH
OFF_AV = OFF_AK + FOX_WIDTH
OFF_AF = OFF_AV + FOX_WIDTH
N_IN = OFF_AF + FOX_HEADS

kernel_name = 'hybrid_mlstm_pool_fox_macaron'


def rms_norm(x, g):
    xf = x.astype(jnp.float32)
    y = xf * lax.rsqrt(jnp.mean(xf * xf, axis=-1, keepdims=True) + RMS_EPS)
    return (y * g.astype(jnp.float32)).astype(x.dtype)


def swiglu(h, w_gate, w_up, w_down):
    return (jax.nn.silu(h @ w_gate) * (h @ w_up)) @ w_down


def split_heads(t, n_heads):
    b, s, _ = t.shape
    return t.reshape(b, s, n_heads, HEAD_DIM).transpose(0, 2, 1, 3)


def merge_heads(t):
    b, h, s, d = t.shape
    return t.transpose(0, 2, 1, 3).reshape(b, s, h * d)


def causal_short_conv(u, w):
    s = u.shape[1]
    y = u * w[0]
    for j in range(1, CONV_WIDTH):
        y = y + jnp.pad(u, ((0, 0), (j, 0), (0, 0)))[:, :s] * w[j]
    return jax.nn.silu(y)


def mlstm_chunkwise(q, k, v, i_pre, log_f):
    b, nh, s, dh = q.shape
    nc = s // MLSTM_CHUNK
    L = MLSTM_CHUNK
    f32 = jnp.float32
    q = q.astype(f32) * (dh ** -0.5)
    to_chunks = lambda t: t.astype(f32).reshape(b, nh, nc, L, dh).transpose(2, 0, 1, 3, 4)
    to_chunks_g = lambda t: t.astype(f32).reshape(b, nh, nc, L).transpose(2, 0, 1, 3)
    xs = (to_chunks(q), to_chunks(k), to_chunks(v), to_chunks_g(i_pre), to_chunks_g(log_f))
    tri = jnp.tril(jnp.ones((L, L), dtype=bool))

    def step(carry, inp):
        C, n, m = carry
        qc, kc, vc, ic, fc = inp
        bcum = jnp.cumsum(fc, axis=-1)
        log_d = bcum[..., :, None] - bcum[..., None, :] + ic[..., None, :]
        log_d = jnp.where(tri, log_d, -jnp.inf)
        inter = bcum + m[..., None]
        m_t = jnp.maximum(inter, jnp.max(log_d, axis=-1))
        scores = jnp.einsum('bhtd,bhsd->bhts', qc, kc) * jnp.exp(log_d - m_t[..., None])
        inter_w = jnp.exp(inter - m_t)
        num = (jnp.einsum('bhts,bhsd->bhtd', scores, vc)
               + inter_w[..., None] * jnp.einsum('bhtk,bhkv->bhtv', qc, C))
        den = scores.sum(-1) + inter_w * jnp.einsum('bhtk,bhk->bht', qc, n)
        h = num / jnp.maximum(jnp.abs(den), jnp.exp(-m_t))[..., None]
        b_last = bcum[..., -1]
        log_w = b_last[..., None] - bcum + ic
        m_new = jnp.maximum(b_last + m, jnp.max(log_w, axis=-1))
        w = jnp.exp(log_w - m_new[..., None])
        decay = jnp.exp(b_last + m - m_new)
        C_new = decay[..., None, None] * C + jnp.einsum('bhs,bhsk,bhsv->bhkv', w, kc, vc)
        n_new = decay[..., None] * n + jnp.einsum('bhs,bhsk->bhk', w, kc)
        return (C_new, n_new, m_new), h

    init = (jnp.zeros((b, nh, dh, dh), f32), jnp.zeros((b, nh, dh), f32), jnp.zeros((b, nh), f32))
    _, hs = lax.scan(step, init, xs)
    return hs.transpose(1, 2, 0, 3, 4).reshape(b, nh, s, dh)


def multiscale_pool(u, pool_w, pool_scale):
    b, s, _ = u.shape
    uf = u.astype(jnp.float32)
    cs = jnp.cumsum(uf, axis=1)
    count = jnp.arange(1, s + 1, dtype=jnp.float32)
    diffs = []
    for g, win in enumerate(POOL_WINDOWS):
        sl = slice(g * POOL_GROUP_DIM, (g + 1) * POOL_GROUP_DIM)
        csg = cs[..., sl]
        prev = jnp.pad(csg, ((0, 0), (win, 0), (0, 0)))[:, :s]
        mean = (csg - prev) / jnp.minimum(count, float(win))[None, :, None]
        diffs.append(mean - uf[..., sl])
    d = jnp.stack(diffs, axis=2).astype(u.dtype)
    y = jnp.einsum('bsgc,gcd->bsgd', d, pool_w).reshape(b, s, POOL_WIDTH)
    return y * pool_scale


def forgetting_attention(q, k, v, log_f):
    b, nh, s, dh = q.shape
    nb = s // FOX_BLOCK
    c = jnp.cumsum(log_f, axis=-1)
    qb = q.reshape(b, nh, nb, FOX_BLOCK, dh).transpose(2, 0, 1, 3, 4)
    cb = c.reshape(b, nh, nb, FOX_BLOCK).transpose(2, 0, 1, 3)
    starts = jnp.arange(nb, dtype=jnp.int32) * FOX_BLOCK
    k_pos = jnp.arange(s, dtype=jnp.int32)
    scale = dh ** -0.5

    def block(args):
        q_blk, c_blk, start = args
        logits = (jnp.einsum('bhqd,bhkd->bhqk', q_blk, k).astype(jnp.float32) * scale
                  + c_blk[..., :, None] - c[..., None, :])
        q_pos = start + jnp.arange(FOX_BLOCK, dtype=jnp.int32)
        logits = jnp.where(k_pos[None, :] <= q_pos[:, None], logits, -jnp.inf)
        p = jax.nn.softmax(logits, axis=-1)
        return jnp.einsum('bhqk,bhkd->bhqd', p.astype(v.dtype), v)

    out = lax.map(block, (qb, cb, starts))
    return out.transpose(1, 2, 0, 3, 4).reshape(b, nh, s, dh)


def hybrid_mixer(h, w_in, mlstm_conv, mlstm_b_i, mlstm_b_f, mlstm_head_g,
                 pool_w, pool_scale, fox_b_f, w_out):
    p = h @ w_in
    qk = causal_short_conv(p[..., OFF_MQ:OFF_MV], mlstm_conv)
    mq, mk = qk[..., :MLSTM_WIDTH], qk[..., MLSTM_WIDTH:]
    mv = p[..., OFF_MV:OFF_MO]
    mo = p[..., OFF_MO:OFF_MI]
    mi = (p[..., OFF_MI:OFF_MF] + mlstm_b_i).astype(jnp.float32).transpose(0, 2, 1)
    mf = jax.nn.log_sigmoid((p[..., OFF_MF:OFF_POOL] + mlstm_b_f).astype(jnp.float32)).transpose(0, 2, 1)
    hm = mlstm_chunkwise(split_heads(mq, MLSTM_HEADS), split_heads(mk, MLSTM_HEADS),
                         split_heads(mv, MLSTM_HEADS), mi, mf)
    hm = hm * lax.rsqrt(jnp.mean(hm * hm, axis=-1, keepdims=True) + RMS_EPS)
    hm = merge_heads(hm).astype(h.dtype) * mlstm_head_g * jax.nn.sigmoid(mo)
    hp = multiscale_pool(p[..., OFF_POOL:OFF_AQ], pool_w, pool_scale)
    aq = split_heads(p[..., OFF_AQ:OFF_AK], FOX_HEADS)
    ak = split_heads(p[..., OFF_AK:OFF_AV], FOX_HEADS)
    av = split_heads(p[..., OFF_AV:OFF_AF], FOX_HEADS)
    af = jax.nn.log_sigmoid((p[..., OFF_AF:N_IN] + fox_b_f).astype(jnp.float32)).transpose(0, 2, 1)
    ha = merge_heads(forgetting_attention(aq, ak, av, af))
    return jnp.concatenate([hm, hp.astype(h.dtype), ha.astype(h.dtype)], axis=-1) @ w_out


def setup_inputs(seed: int = 0) -> dict:
    key = jax.random.key(seed)
    ks = jax.random.split(key, 24)
    f32 = jnp.float32

    def w(k, shape, fan_in):
        return jax.random.normal(k, shape, f32) * (fan_in ** -0.5)

    def gain(k, shape):
        return 1.0 + 0.05 * jax.random.normal(k, shape, f32)

    return {
        'x': jax.random.normal(ks[0], (BATCH, SEQ, D_MODEL), f32),
        'ffn1_pre_g': gain(ks[1], (DEPTH, D_MODEL)),
        'ffn1_post_g': gain(ks[2], (DEPTH, D_MODEL)),
        'ffn1_w_gate': w(ks[3], (DEPTH, D_MODEL, D_FF), D_MODEL),
        'ffn1_w_up': w(ks[4], (DEPTH, D_MODEL, D_FF), D_MODEL),
        'ffn1_w_down': w(ks[5], (DEPTH, D_FF, D_MODEL), D_FF),
        'mix_pre_g': gain(ks[6], (DEPTH, D_MODEL)),
        'mix_post_g': gain(ks[7], (DEPTH, D_MODEL)),
        'w_in': w(ks[8], (DEPTH, D_MODEL, N_IN), D_MODEL),
        'mlstm_conv': w(ks[9], (DEPTH, CONV_WIDTH, 2 * MLSTM_WIDTH), CONV_WIDTH),
        'mlstm_b_i': 0.1 * jax.random.normal(ks[10], (DEPTH, MLSTM_HEADS), f32),
        'mlstm_b_f': jax.random.uniform(ks[11], (DEPTH, MLSTM_HEADS), f32, 3.0, 6.0),
        'mlstm_head_g': gain(ks[12], (DEPTH, MLSTM_WIDTH)),
        'pool_w': w(ks[13], (DEPTH, POOL_GROUPS, POOL_GROUP_DIM, POOL_GROUP_DIM), POOL_GROUP_DIM),
        'pool_scale': gain(ks[14], (DEPTH, POOL_WIDTH)),
        'fox_b_f': jax.random.uniform(ks[15], (DEPTH, FOX_HEADS), f32, 1.0, 6.0),
        'w_out': w(ks[16], (DEPTH, D_MODEL, D_MODEL), D_MODEL),
        'ffn2_pre_g': gain(ks[17], (DEPTH, D_MODEL)),
        'ffn2_post_g': gain(ks[18], (DEPTH, D_MODEL)),
        'ffn2_w_gate': w(ks[19], (DEPTH, D_MODEL, D_FF), D_MODEL),
        'ffn2_w_up': w(ks[20], (DEPTH, D_MODEL, D_FF), D_MODEL),
        'ffn2_w_down': w(ks[21], (DEPTH, D_FF, D_MODEL), D_FF),
    }


def reference(x, ffn1_pre_g, ffn1_post_g, ffn1_w_gate, ffn1_w_up, ffn1_w_down,
              mix_pre_g, mix_post_g, w_in, mlstm_conv, mlstm_b_i, mlstm_b_f, mlstm_head_g,
              pool_w, pool_scale, fox_b_f, w_out,
              ffn2_pre_g, ffn2_post_g, ffn2_w_gate, ffn2_w_up, ffn2_w_down):
    for l in range(DEPTH):
        h = swiglu(rms_norm(x, ffn1_pre_g[l]), ffn1_w_gate[l], ffn1_w_up[l], ffn1_w_down[l])
        x = x + FFN_RESIDUAL_WEIGHT * rms_norm(h, ffn1_post_g[l])
        h = hybrid_mixer(rms_norm(x, mix_pre_g[l]), w_in[l], mlstm_conv[l], mlstm_b_i[l],
                         mlstm_b_f[l], mlstm_head_g[l], pool_w[l], pool_scale[l],
                         fox_b_f[l], w_out[l])
        x = x + rms_norm(h, mix_post_g[l])
        h = swiglu(rms_norm(x, ffn2_pre_g[l]), ffn2_w_gate[l], ffn2_w_up[l], ffn2_w_down[l])
        x = x + FFN_RESIDUAL_WEIGHT * rms_norm(h, ffn2_post_g[l])
    return x
```

```python
import functools

import jax
import jax.numpy as jnp
from jax import lax
from jax.experimental import pallas as pl
from jax.experimental.pallas import tpu as pltpu

F32 = jnp.float32
BF16 = jnp.bfloat16

D_MODEL = 2048
HEAD_DIM = 128
MLSTM_WIDTH = D_MODEL // 4
MLSTM_HEADS = MLSTM_WIDTH // HEAD_DIM
POOL_WIDTH = D_MODEL // 4
POOL_WINDOWS = (2, 4, 8, 16)
POOL_GROUP_DIM = POOL_WIDTH // len(POOL_WINDOWS)
FOX_WIDTH = D_MODEL - MLSTM_WIDTH - POOL_WIDTH
FOX_HEADS = FOX_WIDTH // HEAD_DIM
CONV_WIDTH = 4
MLSTM_CHUNK = 128
RMS_EPS = 1e-6
FFN_RESIDUAL_WEIGHT = 0.5

OFF_MI = 4 * MLSTM_WIDTH
OFF_MF = OFF_MI + MLSTM_HEADS
OFF_POOL = OFF_MF + MLSTM_HEADS
OFF_AQ = OFF_POOL + POOL_WIDTH
OFF_AF = OFF_AQ + 3 * FOX_WIDTH
N_IN = OFF_AF + FOX_HEADS

PM_WIDTH = 4 * MLSTM_WIDTH + POOL_WIDTH
PA_WIDTH = 3 * FOX_WIDTH
N_GATES = 2 * MLSTM_HEADS + FOX_HEADS
LANES = 128
SUBLANES = 8
VMEM_LIMIT_CAP = 58 * 1024 * 1024


def _vmem_limit(block_bytes, scratch_bytes, temp_bytes):
    return int(min(2 * block_bytes + scratch_bytes + temp_bytes + (4 << 20), VMEM_LIMIT_CAP))


def _nbytes(shape, dtype):
    n = 1
    for s in shape:
        n *= s
    return n * jnp.dtype(dtype).itemsize


def _rms_scale(x):
    return lax.rsqrt(jnp.mean(x * x, axis=-1, keepdims=True) + RMS_EPS)


def _ffn_kernel(x_ref, pre_g_ref, post_g_ref, wg_ref, wu_ref, wd_ref, o_ref, h_ref, acc_ref,
                *, n_f, row_chunk):
    j = pl.program_id(1)
    tm = x_ref.shape[0]

    @pl.when(j == 0)
    def _():
        def body(r, c):
            rows = pl.ds(pl.multiple_of(r * row_chunk, row_chunk), row_chunk)
            x = x_ref[rows, :]
            h_ref[rows, :] = (x * _rms_scale(x) * pre_g_ref[...]).astype(BF16)
            return c
        lax.fori_loop(0, tm // row_chunk, body, 0)
        acc_ref[...] = jnp.zeros(acc_ref.shape, F32)

    h = h_ref[...]
    g = jnp.dot(h, wg_ref[...], preferred_element_type=F32)
    u = jnp.dot(h, wu_ref[...], preferred_element_type=F32)
    a = (g * jax.nn.sigmoid(g) * u).astype(BF16)
    acc_ref[...] += jnp.dot(a, wd_ref[...], preferred_element_type=F32)

    @pl.when(j == n_f - 1)
    def _():
        def body(r, c):
            rows = pl.ds(pl.multiple_of(r * row_chunk, row_chunk), row_chunk)
            y = acc_ref[rows, :]
            yn = y * _rms_scale(y) * post_g_ref[...]
            o_ref[rows, :] = x_ref[rows, :] + FFN_RESIDUAL_WEIGHT * yn
            return c
        lax.fori_loop(0, tm // row_chunk, body, 0)


def _ffn(x, pre_g, post_g, wg, wu, wd, *, tm, tf):
    t, d = x.shape
    f = wg.shape[1]
    assert t % tm == 0 and f % tf == 0
    blocks = (_nbytes((tm, d), F32) * 2 + 2 * _nbytes((d, tf), BF16) + _nbytes((tf, d), BF16))
    scratch = _nbytes((tm, d), BF16) + _nbytes((tm, d), F32)
    temps = 4 * _nbytes((tm, tf), F32) + _nbytes((tm, d), F32)
    return pl.pallas_call(
        functools.partial(_ffn_kernel, n_f=f // tf, row_chunk=64),
        grid=(t // tm, f // tf),
        in_specs=[
            pl.BlockSpec((tm, d), lambda i, j: (i, 0)),
            pl.BlockSpec((1, d), lambda i, j: (0, 0)),
            pl.BlockSpec((1, d), lambda i, j: (0, 0)),
            pl.BlockSpec((d, tf), lambda i, j: (0, j)),
            pl.BlockSpec((d, tf), lambda i, j: (0, j)),
            pl.BlockSpec((tf, d), lambda i, j: (j, 0)),
        ],
        out_specs=pl.BlockSpec((tm, d), lambda i, j: (i, 0)),
        out_shape=jax.ShapeDtypeStruct((t, d), F32),
        scratch_shapes=[pltpu.VMEM((tm, d), BF16), pltpu.VMEM((tm, d), F32)],
        compiler_params=pltpu.CompilerParams(
            dimension_semantics=("parallel", "arbitrary"),
            vmem_limit_bytes=_vmem_limit(blocks, scratch, temps)),
        name="ffn",
    )(x, pre_g.reshape(1, d), post_g.reshape(1, d), wg, wu, wd)


def _proj_in_kernel(x_ref, g_ref, wm_ref, wgate_ref, pm_ref, pa_ref, pg_ref, h_ref,
                    *, n_pm, row_chunk):
    j = pl.program_id(1)
    tm = x_ref.shape[0]

    @pl.when(j == 0)
    def _():
        def body(r, c):
            rows = pl.ds(pl.multiple_of(r * row_chunk, row_chunk), row_chunk)
            x = x_ref[rows, :]
            h_ref[rows, :] = (x * _rms_scale(x) * g_ref[...]).astype(BF16)
            return c
        lax.fori_loop(0, tm // row_chunk, body, 0)
        pg_ref[...] = jnp.dot(h_ref[...], wgate_ref[...], preferred_element_type=F32)

    p = jnp.dot(h_ref[...], wm_ref[...], preferred_element_type=F32)

    @pl.when(j < n_pm)
    def _():
        pm_ref[...] = p

    @pl.when(j >= n_pm)
    def _():
        pa_ref[...] = p.astype(BF16)


def _proj_in(x, g, w_main, w_gate, *, tm, tn):
    t, d = x.shape
    n = w_main.shape[1]
    assert t % tm == 0 and n % tn == 0 and PM_WIDTH % tn == 0 and n == PM_WIDTH + PA_WIDTH
    n_pm = PM_WIDTH // tn
    blocks = (_nbytes((tm, d), F32) + _nbytes((d, tn), BF16) + _nbytes((d, LANES), BF16)
              + _nbytes((tm, tn), F32) + _nbytes((tm, tn), BF16) + _nbytes((tm, LANES), F32))
    scratch = _nbytes((tm, d), BF16)
    temps = 2 * _nbytes((tm, tn), F32)
    return pl.pallas_call(
        functools.partial(_proj_in_kernel, n_pm=n_pm, row_chunk=64),
        grid=(t // tm, n // tn),
        in_specs=[
            pl.BlockSpec((tm, d), lambda i, j: (i, 0)),
            pl.BlockSpec((1, d), lambda i, j: (0, 0)),
            pl.BlockSpec((d, tn), lambda i, j: (0, j)),
            pl.BlockSpec((d, LANES), lambda i, j: (0, 0)),
        ],
        out_specs=[
            pl.BlockSpec((tm, tn), lambda i, j: (i, jnp.minimum(j, n_pm - 1))),
            pl.BlockSpec((tm, tn), lambda i, j: (i, jnp.maximum(j - n_pm, 0))),
            pl.BlockSpec((tm, LANES), lambda i, j: (i, 0)),
        ],
        out_shape=[
            jax.ShapeDtypeStruct((t, PM_WIDTH), F32),
            jax.ShapeDtypeStruct((t, PA_WIDTH), BF16),
            jax.ShapeDtypeStruct((t, LANES), F32),
        ],
        scratch_shapes=[pltpu.VMEM((tm, d), BF16)],
        compiler_params=pltpu.CompilerParams(
            dimension_semantics=("parallel", "arbitrary"),
            vmem_limit_bytes=_vmem_limit(blocks, scratch, temps)),
        name="proj_in",
    )(x, g.reshape(1, d), w_main, w_gate)


def _log_sigmoid(z):
    return -(jnp.maximum(-z, 0.0) + jnp.log1p(jnp.exp(-jnp.abs(z))))


def _lane_scan(v, shifts, lane_pos):
    for sh in shifts:
        v = v + jnp.where(lane_pos >= sh, pltpu.roll(v, sh, axis=1), 0.0)
    return v


def _gates_kernel(pg_ref, bias_ref, row_ref, col_ref):
    s = pg_ref.shape[0]
    z = pg_ref[...].T[0:N_GATES, :] + bias_ref[...]
    lane = lax.broadcasted_iota(jnp.int32, (1, s), 1)
    mi = z[0:MLSTM_HEADS]
    lf_m = _log_sigmoid(z[MLSTM_HEADS:2 * MLSTM_HEADS])
    lf_a = _log_sigmoid(z[2 * MLSTM_HEADS:N_GATES])
    chunk_shifts = [1 << k for k in range(MLSTM_CHUNK.bit_length() - 1)]
    seq_shifts = [1 << k for k in range((s - 1).bit_length())]
    bcum = _lane_scan(lf_m, chunk_shifts, lane % MLSTM_CHUNK)
    cfox = _lane_scan(lf_a, seq_shifts, lane)
    rows = jnp.concatenate([mi, bcum, cfox], axis=0)
    row_ref[0] = rows
    padded = jnp.concatenate([rows, jnp.zeros((LANES - N_GATES, s), F32)], axis=0)
    col_ref[...] = padded.T


def _gates(pg, bias, *, batch, seq):
    t = pg.shape[0]
    assert t == batch * seq
    blocks = 2 * _nbytes((seq, LANES), F32) + _nbytes((N_GATES, seq), F32)
    temps = 6 * _nbytes((seq, LANES), F32)
    return pl.pallas_call(
        _gates_kernel,
        grid=(batch,),
        in_specs=[
            pl.BlockSpec((seq, LANES), lambda b: (b, 0)),
            pl.BlockSpec((N_GATES, 1), lambda b: (0, 0)),
        ],
        out_specs=[
            pl.BlockSpec((1, N_GATES, seq), lambda b: (b, 0, 0)),
            pl.BlockSpec((seq, LANES), lambda b: (b, 0)),
        ],
        out_shape=[
            jax.ShapeDtypeStruct((batch, N_GATES, seq), F32),
            jax.ShapeDtypeStruct((t, LANES), F32),
        ],
        compiler_params=pltpu.CompilerParams(
            dimension_semantics=("parallel",),
            vmem_limit_bytes=_vmem_limit(blocks, 0, temps)),
        name="gates",
    )(pg, bias)


def _mlstm_kernel(qk_ref, v_ref, o_ref, gcol_ref, grow_ref, convw_ref, hg_ref, out_ref,
                  ubuf, qkc, cn_ref, m_ref, *, tb):
    step = pl.program_id(1)
    L = MLSTM_CHUNK
    halo = SUBLANES

    @pl.when(step == 0)
    def _():
        ubuf[0:halo, :] = jnp.zeros((halo, 2 * MLSTM_WIDTH), F32)
        cn_ref[...] = jnp.zeros(cn_ref.shape, F32)
        m_ref[...] = jnp.zeros(m_ref.shape, F32)

    @pl.when(step > 0)
    def _():
        ubuf[0:halo, :] = ubuf[tb:tb + halo, :]

    ubuf[halo:halo + tb, :] = qk_ref[...]
    w = convw_ref[...]
    y = ubuf[halo:halo + tb, :] * w[0:1, :]
    for j in range(1, CONV_WIDTH):
        y = y + ubuf[halo - j:halo - j + tb, :] * w[j:j + 1, :]
    qkc[...] = y * jax.nn.sigmoid(y)

    tri = (lax.broadcasted_iota(jnp.int32, (L, L), 0) >= lax.broadcasted_iota(jnp.int32, (L, L), 1))
    ones_col = (lax.broadcasted_iota(jnp.int32, (L, HEAD_DIM), 1) == 0).astype(F32)
    q_scale = HEAD_DIM ** -0.5

    for hd in range(MLSTM_HEADS):
        hs = slice(hd * HEAD_DIM, (hd + 1) * HEAD_DIM)
        ks = slice(MLSTM_WIDTH + hd * HEAD_DIM, MLSTM_WIDTH + (hd + 1) * HEAD_DIM)
        for c in range(tb // L):
            rows = slice(c * L, (c + 1) * L)
            q = (qkc[rows, hs] * q_scale).astype(BF16)
            k = qkc[rows, ks]
            k_bf = k.astype(BF16)
            vaug = jnp.concatenate([v_ref[rows, hs], ones_col], axis=1)
            i_col = gcol_ref[rows, hd:hd + 1]
            b_col = gcol_ref[rows, MLSTM_HEADS + hd:MLSTM_HEADS + hd + 1]
            i_row = grow_ref[0, hd:hd + 1, rows]
            b_row = grow_ref[0, MLSTM_HEADS + hd:MLSTM_HEADS + hd + 1, rows]
            m_prev = m_ref[hd][0:1, 0:1]
            cn = cn_ref[hd]

            log_d = jnp.where(tri, b_col - b_row + i_row, -jnp.inf)
            inter = b_col + m_prev
            m_t = jnp.maximum(inter, jnp.max(log_d, axis=-1, keepdims=True))
            s = lax.dot_general(q, k_bf, (((1,), (1,)), ((), ())), preferred_element_type=F32)
            scores = (s * jnp.exp(log_d - m_t)).astype(BF16)
            inter_w = jnp.exp(inter - m_t)
            numden = (jnp.dot(scores, vaug.astype(BF16), preferred_element_type=F32)
                      + inter_w * jnp.dot(q, cn.astype(BF16), preferred_element_type=F32))
            num = numden[:, 0:HEAD_DIM]
            den = numden[:, HEAD_DIM:HEAD_DIM + 1]
            h = num / jnp.maximum(jnp.abs(den), jnp.exp(-m_t))

            b_last = b_row[:, L - 1:L]
            log_w = b_last - b_col + i_col
            m_new = jnp.maximum(b_last + m_prev, jnp.max(log_w, axis=0, keepdims=True))
            wv = (jnp.exp(log_w - m_new) * vaug).astype(BF16)
            decay = jnp.exp(b_last + m_prev - m_new)
            cn_ref[hd] = decay * cn + jnp.dot(k.T.astype(BF16), wv, preferred_element_type=F32)
            m_ref[hd] = jnp.broadcast_to(m_new, (SUBLANES, LANES))

            hn = h * _rms_scale(h)
            out_ref[rows, hs] = (hn * hg_ref[:, hs] * jax.nn.sigmoid(o_ref[rows, hs])).astype(BF16)


def _mlstm(pm, gcol, grow, conv_w, head_g, *, batch, seq, tb):
    t = pm.shape[0]
    assert seq % tb == 0 and tb % MLSTM_CHUNK == 0
    nb = seq // tb
    w2 = 2 * MLSTM_WIDTH
    blocks = (_nbytes((tb, w2), F32) + 2 * _nbytes((tb, MLSTM_WIDTH), F32) + _nbytes((tb, LANES), F32)
              + _nbytes((N_GATES, tb), F32) + _nbytes((tb, MLSTM_WIDTH), BF16))
    scratch = (_nbytes((tb + 2 * SUBLANES, w2), F32) + _nbytes((tb, w2), F32)
               + _nbytes((MLSTM_HEADS, HEAD_DIM, 2 * HEAD_DIM), F32))
    temps = 2 * _nbytes((tb, w2), F32)
    return pl.pallas_call(
        functools.partial(_mlstm_kernel, tb=tb),
        grid=(batch, nb),
        in_specs=[
            pl.BlockSpec((tb, w2), lambda b, i: (b * nb + i, 0)),
            pl.BlockSpec((tb, MLSTM_WIDTH), lambda b, i: (b * nb + i, 2)),
            pl.BlockSpec((tb, MLSTM_WIDTH), lambda b, i: (b * nb + i, 3)),
            pl.BlockSpec((tb, LANES), lambda b, i: (b * nb + i, 0)),
            pl.BlockSpec((1, N_GATES, tb), lambda b, i: (b, 0, i)),
            pl.BlockSpec((CONV_WIDTH, w2), lambda b, i: (0, 0)),
            pl.BlockSpec((1, MLSTM_WIDTH), lambda b, i: (0, 0)),
        ],
        out_specs=pl.BlockSpec((tb, MLSTM_WIDTH), lambda b, i: (b * nb + i, 0)),
        out_shape=jax.ShapeDtypeStruct((t, MLSTM_WIDTH), BF16),
        scratch_shapes=[
            pltpu.VMEM((tb + 2 * SUBLANES, w2), F32),
            pltpu.VMEM((tb, w2), F32),
            pltpu.VMEM((MLSTM_HEADS, HEAD_DIM, 2 * HEAD_DIM), F32),
            pltpu.VMEM((MLSTM_HEADS, SUBLANES, LANES), F32),
        ],
        compiler_params=pltpu.CompilerParams(
            dimension_semantics=("parallel", "arbitrary"),
            vmem_limit_bytes=_vmem_limit(blocks, scratch, temps)),
        name="mlstm",
    )(pm, pm, pm, gcol, grow, conv_w, head_g.reshape(1, MLSTM_WIDTH))


def _pool_kernel(u_ref, w_ref, scale_ref, out_ref, ext, *, ts):
    step = pl.program_id(1)
    halo = 2 * SUBLANES
    assert halo >= max(POOL_WINDOWS)

    @pl.when(step == 0)
    def _():
        ext[0:halo, :] = jnp.zeros((halo, POOL_WIDTH), F32)

    @pl.when(step > 0)
    def _():
        ext[0:halo, :] = ext[ts:ts + halo, :]

    ext[halo:halo + ts, :] = u_ref[...]
    pos = (step * ts + lax.broadcasted_iota(jnp.int32, (ts, 1), 0) + 1).astype(F32)
    for g, win in enumerate(POOL_WINDOWS):
        cols = slice(g * POOL_GROUP_DIM, (g + 1) * POOL_GROUP_DIM)
        acc = ext[halo:halo + ts, cols]
        for j in range(1, win):
            acc = acc + ext[halo - j:halo - j + ts, cols]
        mean = acc / jnp.minimum(pos, float(win))
        diff = (mean - ext[halo:halo + ts, cols]).astype(BF16)
        y = jnp.dot(diff, w_ref[g], preferred_element_type=F32)
        out_ref[:, cols] = (y * scale_ref[:, cols]).astype(BF16)


def _pool(pm, pool_w, pool_scale, *, batch, seq, ts):
    t = pm.shape[0]
    assert seq % ts == 0
    nb = seq // ts
    col_block = (4 * MLSTM_WIDTH) // POOL_WIDTH
    blocks = (_nbytes((ts, POOL_WIDTH), F32) + _nbytes(pool_w.shape, BF16)
              + _nbytes((ts, POOL_WIDTH), BF16))
    scratch = _nbytes((ts + 2 * SUBLANES, POOL_WIDTH), F32)
    temps = 4 * _nbytes((ts, POOL_GROUP_DIM), F32)
    return pl.pallas_call(
        functools.partial(_pool_kernel, ts=ts),
        grid=(batch, nb),
        in_specs=[
            pl.BlockSpec((ts, POOL_WIDTH), lambda b, i: (b * nb + i, col_block)),
            pl.BlockSpec(pool_w.shape, lambda b, i: (0, 0, 0)),
            pl.BlockSpec((1, POOL_WIDTH), lambda b, i: (0, 0)),
        ],
        out_specs=pl.BlockSpec((ts, POOL_WIDTH), lambda b, i: (b * nb + i, 0)),
        out_shape=jax.ShapeDtypeStruct((t, POOL_WIDTH), BF16),
        scratch_shapes=[pltpu.VMEM((ts + 2 * SUBLANES, POOL_WIDTH), F32)],
        compiler_params=pltpu.CompilerParams(
            dimension_semantics=("parallel", "arbitrary"),
            vmem_limit_bytes=_vmem_limit(blocks, scratch, temps)),
        name="pool",
    )(pm, pool_w, pool_scale.reshape(1, POOL_WIDTH))


def _fox_kernel(q_ref, k_ref, v_ref, cq_ref, ck_ref, out_ref, m_ref, l_ref, acc_ref, *, tq):
    qi = pl.program_id(2)
    scale = HEAD_DIM ** -0.5
    q = q_ref[...]
    cq = jnp.broadcast_to(cq_ref[0], (LANES, tq)).T[:, 0:1]

    m_ref[...] = jnp.full(m_ref.shape, -jnp.inf, F32)
    l_ref[...] = jnp.zeros(l_ref.shape, F32)
    acc_ref[...] = jnp.zeros(acc_ref.shape, F32)

    def update(j, masked):
        rows = pl.ds(pl.multiple_of(j * tq, tq), tq)
        s = lax.dot_general(q, k_ref[rows, :], (((1,), (1,)), ((), ())),
                            preferred_element_type=F32)
        logits = s * scale + cq - ck_ref[0, pl.ds(j, 1), :]
        if masked:
            keep = (lax.broadcasted_iota(jnp.int32, (tq, tq), 1)
                    <= lax.broadcasted_iota(jnp.int32, (tq, tq), 0))
            logits = jnp.where(keep, logits, -jnp.inf)
        m_prev = m_ref[...]
        m_new = jnp.maximum(m_prev, jnp.max(logits, axis=-1, keepdims=True))
        alpha = jnp.exp(m_prev - m_new)
        p = jnp.exp(logits - m_new)
        l_ref[...] = alpha * l_ref[...] + jnp.sum(p, axis=-1, keepdims=True)
        acc_ref[...] = alpha * acc_ref[...] + jnp.dot(p.astype(BF16), v_ref[rows, :],
                                                      preferred_element_type=F32)
        m_ref[...] = m_new

    def body(j, c):
        update(j, False)
        return c

    lax.fori_loop(0, qi, body, 0)
    update(qi, True)
    out_ref[...] = (acc_ref[...] / l_ref[...]).astype(BF16)


def _fox(pa, grow, *, batch, seq, tq):
    t = pa.shape[0]
    assert seq % tq == 0
    nq = seq // tq
    ck = grow.reshape(batch * N_GATES, nq, tq)
    cq = grow.reshape(batch * N_GATES, 1, seq)
    gate0 = 2 * MLSTM_HEADS
    blocks = (2 * _nbytes((tq, HEAD_DIM), BF16) + 2 * _nbytes((seq, HEAD_DIM), BF16)
              + _nbytes((SUBLANES, tq), F32) + _nbytes((nq, tq), F32))
    scratch = _nbytes((tq, HEAD_DIM), F32) + 2 * _nbytes((tq, LANES), F32)
    temps = 6 * _nbytes((tq, tq), F32)
    return pl.pallas_call(
        functools.partial(_fox_kernel, tq=tq),
        grid=(batch, FOX_HEADS, nq),
        in_specs=[
            pl.BlockSpec((tq, HEAD_DIM), lambda b, h, i: (b * nq + i, h)),
            pl.BlockSpec((seq, HEAD_DIM), lambda b, h, i: (b, FOX_HEADS + h)),
            pl.BlockSpec((seq, HEAD_DIM), lambda b, h, i: (b, 2 * FOX_HEADS + h)),
            pl.BlockSpec((1, 1, tq), lambda b, h, i: (b * N_GATES + gate0 + h, 0, i)),
            pl.BlockSpec((1, nq, tq), lambda b, h, i: (b * N_GATES + gate0 + h, 0, 0)),
        ],
        out_specs=pl.BlockSpec((tq, HEAD_DIM), lambda b, h, i: (b * nq + i, h)),
        out_shape=jax.ShapeDtypeStruct((t, FOX_WIDTH), BF16),
        scratch_shapes=[
            pltpu.VMEM((tq, 1), F32),
            pltpu.VMEM((tq, 1), F32),
            pltpu.VMEM((tq, HEAD_DIM), F32),
        ],
        compiler_params=pltpu.CompilerParams(
            dimension_semantics=("parallel", "parallel", "arbitrary"),
            vmem_limit_bytes=_vmem_limit(blocks, scratch, temps)),
        name="fox",
    )(pa, pa, pa, cq, ck)


def _proj_out_kernel(hm_ref, hp_ref, ha_ref, w_ref, x_ref, g_ref, o_ref, *, row_chunk):
    tm = x_ref.shape[0]
    y = jnp.dot(hm_ref[...], w_ref[0:MLSTM_WIDTH, :], preferred_element_type=F32)
    y = y + jnp.dot(hp_ref[...], w_ref[MLSTM_WIDTH:MLSTM_WIDTH + POOL_WIDTH, :],
                    preferred_element_type=F32)
    y = y + jnp.dot(ha_ref[...], w_ref[MLSTM_WIDTH + POOL_WIDTH:, :], preferred_element_type=F32)
    o_ref[...] = y

    def body(r, c):
        rows = pl.ds(pl.multiple_of(r * row_chunk, row_chunk), row_chunk)
        yr = o_ref[rows, :]
        o_ref[rows, :] = x_ref[rows, :] + yr * _rms_scale(yr) * g_ref[...]
        return c
    lax.fori_loop(0, tm // row_chunk, body, 0)


def _proj_out(hm, hp, ha, w_out, x, g, *, tm):
    t, d = x.shape
    assert t % tm == 0
    blocks = (_nbytes((tm, d), BF16) + _nbytes((d, d), BF16) + 2 * _nbytes((tm, d), F32))
    temps = 2 * _nbytes((tm, d), F32)
    return pl.pallas_call(
        functools.partial(_proj_out_kernel, row_chunk=64),
        grid=(t // tm,),
        in_specs=[
            pl.BlockSpec((tm, MLSTM_WIDTH), lambda i: (i, 0)),
            pl.BlockSpec((tm, POOL_WIDTH), lambda i: (i, 0)),
            pl.BlockSpec((tm, FOX_WIDTH), lambda i: (i, 0)),
            pl.BlockSpec((d, d), lambda i: (0, 0)),
            pl.BlockSpec((tm, d), lambda i: (i, 0)),
            pl.BlockSpec((1, d), lambda i: (0, 0)),
        ],
        out_specs=pl.BlockSpec((tm, d), lambda i: (i, 0)),
        out_shape=jax.ShapeDtypeStruct((t, d), F32),
        compiler_params=pltpu.CompilerParams(
            dimension_semantics=("parallel",),
            vmem_limit_bytes=_vmem_limit(blocks, 0, temps)),
        name="proj_out",
    )(hm, hp, ha, w_out, x, g.reshape(1, d))


def _mixer(x, pre_g, post_g, w_in, conv_w, b_i, b_f, head_g, pool_w, pool_scale, fox_b_f, w_out,
           *, batch, seq):
    w_main = jnp.concatenate([w_in[:, :OFF_MI], w_in[:, OFF_POOL:OFF_AF]], axis=1).astype(BF16)
    w_gate = jnp.concatenate([w_in[:, OFF_MI:OFF_POOL], w_in[:, OFF_AF:N_IN]], axis=1)
    w_gate = jnp.pad(w_gate, ((0, 0), (0, LANES - N_GATES))).astype(BF16)
    bias = jnp.concatenate([b_i, b_f, fox_b_f]).reshape(N_GATES, 1)

    pm, pa, pg = _proj_in(x, pre_g, w_main, w_gate, tm=512, tn=512)
    grow, gcol = _gates(pg, bias, batch=batch, seq=seq)
    hm = _mlstm(pm, gcol, grow, conv_w, head_g, batch=batch, seq=seq, tb=min(512, seq))
    hp = _pool(pm, pool_w.astype(BF16), pool_scale, batch=batch, seq=seq, ts=min(512, seq))
    ha = _fox(pa, grow, batch=batch, seq=seq, tq=min(512, seq))
    return _proj_out(hm, hp, ha, w_out.astype(BF16), x, post_g, tm=512)


def kernel(x, ffn1_pre_g, ffn1_post_g, ffn1_w_gate, ffn1_w_up, ffn1_w_down, mix_pre_g, mix_post_g,
           w_in, mlstm_conv, mlstm_b_i, mlstm_b_f, mlstm_head_g, pool_w, pool_scale, fox_b_f, w_out,
           ffn2_pre_g, ffn2_post_g, ffn2_w_gate, ffn2_w_up, ffn2_w_down):
    batch, seq, d = x.shape
    depth = w_in.shape[0]
    xt = x.reshape(batch * seq, d)
    for l in range(depth):
        xt = _ffn(xt, ffn1_pre_g[l], ffn1_post_g[l], ffn1_w_gate[l].astype(BF16),
                  ffn1_w_up[l].astype(BF16), ffn1_w_down[l].astype(BF16), tm=512, tf=512)
        xt = _mixer(xt, mix_pre_g[l], mix_post_g[l], w_in[l], mlstm_conv[l], mlstm_b_i[l],
                    mlstm_b_f[l], mlstm_head_g[l], pool_w[l], pool_scale[l], fox_b_f[l], w_out[l],
                    batch=batch, seq=seq)
        xt = _ffn(xt, ffn2_pre_g[l], ffn2_post_g[l], ffn2_w_gate[l].astype(BF16),
                  ffn2_w_up[l].astype(BF16), ffn2_w_down[l].astype(BF16), tm=512, tf=512)
    return xt.reshape(batch, seq, d)
```

```python
import functools

import jax
import jax.numpy as jnp
from jax import lax
from jax.experimental import pallas as pl
from jax.experimental.pallas import tpu as pltpu

F32 = jnp.float32
BF16 = jnp.bfloat16

D_MODEL = 2048
HEAD_DIM = 128
MLSTM_WIDTH = D_MODEL // 4
MLSTM_HEADS = MLSTM_WIDTH // HEAD_DIM
POOL_WIDTH = D_MODEL // 4
POOL_WINDOWS = (2, 4, 8, 16)
POOL_GROUP_DIM = POOL_WIDTH // len(POOL_WINDOWS)
FOX_WIDTH = D_MODEL - MLSTM_WIDTH - POOL_WIDTH
FOX_HEADS = FOX_WIDTH // HEAD_DIM
CONV_WIDTH = 4
MLSTM_CHUNK = 128
RMS_EPS = 1e-6
FFN_RESIDUAL_WEIGHT = 0.5

OFF_MI = 4 * MLSTM_WIDTH
OFF_MF = OFF_MI + MLSTM_HEADS
OFF_POOL = OFF_MF + MLSTM_HEADS
OFF_AQ = OFF_POOL + POOL_WIDTH
OFF_AF = OFF_AQ + 3 * FOX_WIDTH
N_IN = OFF_AF + FOX_HEADS

PM_WIDTH = 4 * MLSTM_WIDTH + POOL_WIDTH
PA_WIDTH = 3 * FOX_WIDTH
N_GATES = 2 * MLSTM_HEADS + FOX_HEADS
LOG2E = 1.4426950408889634
FOX_Q_SCALE = LOG2E * HEAD_DIM ** -0.5
LANES = 128
SUBLANES = 8
VMEM_LIMIT_CAP = 58 * 1024 * 1024


def _vmem_limit(block_bytes, scratch_bytes, temp_bytes):
    return int(min(2 * block_bytes + scratch_bytes + temp_bytes + (4 << 20), VMEM_LIMIT_CAP))


def _nbytes(shape, dtype):
    n = 1
    for s in shape:
        n *= s
    return n * jnp.dtype(dtype).itemsize


def _rms_scale(x):
    return lax.rsqrt(jnp.mean(x * x, axis=-1, keepdims=True) + RMS_EPS)


def _ffn_kernel(x_ref, pre_g_ref, post_g_ref, wg_ref, wu_ref, wd_ref, o_ref, h_ref, acc_ref,
                *, n_f, row_chunk):
    j = pl.program_id(1)
    tm = x_ref.shape[0]

    @pl.when(j == 0)
    def _():
        def body(r, c):
            rows = pl.ds(pl.multiple_of(r * row_chunk, row_chunk), row_chunk)
            x = x_ref[rows, :]
            h_ref[rows, :] = (x * _rms_scale(x) * pre_g_ref[...]).astype(BF16)
            return c
        lax.fori_loop(0, tm // row_chunk, body, 0)
        acc_ref[...] = jnp.zeros(acc_ref.shape, F32)

    h = h_ref[...]
    g = jnp.dot(h, wg_ref[...], preferred_element_type=F32)
    u = jnp.dot(h, wu_ref[...], preferred_element_type=F32)
    a = (g * jax.nn.sigmoid(g) * u).astype(BF16)
    acc_ref[...] += jnp.dot(a, wd_ref[...], preferred_element_type=F32)

    @pl.when(j == n_f - 1)
    def _():
        def body(r, c):
            rows = pl.ds(pl.multiple_of(r * row_chunk, row_chunk), row_chunk)
            y = acc_ref[rows, :]
            yn = y * _rms_scale(y) * post_g_ref[...]
            o_ref[rows, :] = x_ref[rows, :] + FFN_RESIDUAL_WEIGHT * yn
            return c
        lax.fori_loop(0, tm // row_chunk, body, 0)


def _ffn(x, pre_g, post_g, wg, wu, wd, *, tm, tf):
    t, d = x.shape
    f = wg.shape[1]
    assert t % tm == 0 and f % tf == 0
    blocks = (_nbytes((tm, d), F32) * 2 + 2 * _nbytes((d, tf), BF16) + _nbytes((tf, d), BF16))
    scratch = _nbytes((tm, d), BF16) + _nbytes((tm, d), F32)
    temps = 4 * _nbytes((tm, tf), F32) + _nbytes((tm, d), F32)
    return pl.pallas_call(
        functools.partial(_ffn_kernel, n_f=f // tf, row_chunk=64),
        grid=(t // tm, f // tf),
        in_specs=[
            pl.BlockSpec((tm, d), lambda i, j: (i, 0)),
            pl.BlockSpec((1, d), lambda i, j: (0, 0)),
            pl.BlockSpec((1, d), lambda i, j: (0, 0)),
            pl.BlockSpec((d, tf), lambda i, j: (0, j)),
            pl.BlockSpec((d, tf), lambda i, j: (0, j)),
            pl.BlockSpec((tf, d), lambda i, j: (j, 0)),
        ],
        out_specs=pl.BlockSpec((tm, d), lambda i, j: (i, 0)),
        out_shape=jax.ShapeDtypeStruct((t, d), F32),
        scratch_shapes=[pltpu.VMEM((tm, d), BF16), pltpu.VMEM((tm, d), F32)],
        compiler_params=pltpu.CompilerParams(
            dimension_semantics=("parallel", "arbitrary"),
            vmem_limit_bytes=_vmem_limit(blocks, scratch, temps)),
        name="ffn",
    )(x, pre_g.reshape(1, d), post_g.reshape(1, d), wg, wu, wd)


def _proj_in_kernel(x_ref, g_ref, wpm_ref, wpa_ref, wgate_ref, pm_ref, pa_ref, pg_ref):
    x = x_ref[...]
    h = (x * _rms_scale(x) * g_ref[...]).astype(BF16)
    pm_ref[...] = jnp.dot(h, wpm_ref[...], preferred_element_type=F32)
    pa = jnp.dot(h, wpa_ref[...], preferred_element_type=F32)
    pa_ref[:, 0:FOX_WIDTH] = (pa[:, 0:FOX_WIDTH] * FOX_Q_SCALE).astype(BF16)
    pa_ref[:, FOX_WIDTH:] = pa[:, FOX_WIDTH:].astype(BF16)
    pg_ref[...] = jnp.dot(h, wgate_ref[...], preferred_element_type=F32)


def _proj_in(x, g, w_pm, w_pa, w_gate, *, tm):
    t, d = x.shape
    assert t % tm == 0 and w_pm.shape == (d, PM_WIDTH) and w_pa.shape == (d, PA_WIDTH)
    resident = pl.Buffered(1)
    blocks = (_nbytes((tm, d), F32) + _nbytes((d, LANES), BF16) + _nbytes((tm, PM_WIDTH), F32)
              + _nbytes((tm, PA_WIDTH), BF16) + _nbytes((tm, LANES), F32))
    scratch = _nbytes((d, PM_WIDTH + PA_WIDTH), BF16)
    temps = _nbytes((tm, d), BF16) + 2 * _nbytes((tm, PA_WIDTH), F32)
    return pl.pallas_call(
        _proj_in_kernel,
        grid=(t // tm,),
        in_specs=[
            pl.BlockSpec((tm, d), lambda i: (i, 0)),
            pl.BlockSpec((1, d), lambda i: (0, 0)),
            pl.BlockSpec((d, PM_WIDTH), lambda i: (0, 0), pipeline_mode=resident),
            pl.BlockSpec((d, PA_WIDTH), lambda i: (0, 0), pipeline_mode=resident),
            pl.BlockSpec((d, LANES), lambda i: (0, 0)),
        ],
        out_specs=[
            pl.BlockSpec((tm, PM_WIDTH), lambda i: (i, 0)),
            pl.BlockSpec((tm, PA_WIDTH), lambda i: (i, 0)),
            pl.BlockSpec((tm, LANES), lambda i: (i, 0)),
        ],
        out_shape=[
            jax.ShapeDtypeStruct((t, PM_WIDTH), F32),
            jax.ShapeDtypeStruct((t, PA_WIDTH), BF16),
            jax.ShapeDtypeStruct((t, LANES), F32),
        ],
        compiler_params=pltpu.CompilerParams(
            dimension_semantics=("parallel",),
            vmem_limit_bytes=_vmem_limit(blocks, scratch, temps)),
        name="proj_in",
    )(x, g.reshape(1, d), w_pm, w_pa, w_gate)


def _log_sigmoid(z):
    return -(jnp.maximum(-z, 0.0) + jnp.log1p(jnp.exp(-jnp.abs(z))))


def _lane_scan(v, shifts, lane_pos):
    for sh in shifts:
        v = v + jnp.where(lane_pos >= sh, pltpu.roll(v, sh, axis=1), 0.0)
    return v


def _gates_kernel(pg_ref, bias_ref, row_ref, col_ref):
    s = pg_ref.shape[0]
    z = pg_ref[...].T[0:N_GATES, :] + bias_ref[...]
    lane = lax.broadcasted_iota(jnp.int32, (1, s), 1)
    mi = z[0:MLSTM_HEADS]
    lf_m = _log_sigmoid(z[MLSTM_HEADS:2 * MLSTM_HEADS])
    lf_a = _log_sigmoid(z[2 * MLSTM_HEADS:N_GATES])
    chunk_shifts = [1 << k for k in range(MLSTM_CHUNK.bit_length() - 1)]
    seq_shifts = [1 << k for k in range((s - 1).bit_length())]
    bcum = _lane_scan(lf_m, chunk_shifts, lane % MLSTM_CHUNK)
    cfox = _lane_scan(lf_a, seq_shifts, lane)
    rows = jnp.concatenate([mi, bcum, cfox], axis=0)
    row_ref[0] = rows
    padded = jnp.concatenate([rows, jnp.zeros((LANES - N_GATES, s), F32)], axis=0)
    col_ref[...] = padded.T


def _gates(pg, bias, *, batch, seq):
    t = pg.shape[0]
    assert t == batch * seq
    blocks = 2 * _nbytes((seq, LANES), F32) + _nbytes((N_GATES, seq), F32)
    temps = 6 * _nbytes((seq, LANES), F32)
    return pl.pallas_call(
        _gates_kernel,
        grid=(batch,),
        in_specs=[
            pl.BlockSpec((seq, LANES), lambda b: (b, 0)),
            pl.BlockSpec((N_GATES, 1), lambda b: (0, 0)),
        ],
        out_specs=[
            pl.BlockSpec((1, N_GATES, seq), lambda b: (b, 0, 0)),
            pl.BlockSpec((seq, LANES), lambda b: (b, 0)),
        ],
        out_shape=[
            jax.ShapeDtypeStruct((batch, N_GATES, seq), F32),
            jax.ShapeDtypeStruct((t, LANES), F32),
        ],
        compiler_params=pltpu.CompilerParams(
            dimension_semantics=("parallel",),
            vmem_limit_bytes=_vmem_limit(blocks, 0, temps)),
        name="gates",
    )(pg, bias)


def _mlstm_kernel(qk_ref, v_ref, o_ref, gcol_ref, grow_ref, convw_ref, hg_ref, out_ref,
                  ubuf, qkc, cn_ref, m_ref, *, tb):
    step = pl.program_id(1)
    L = MLSTM_CHUNK
    halo = SUBLANES

    @pl.when(step == 0)
    def _():
        ubuf[0:halo, :] = jnp.zeros((halo, 2 * MLSTM_WIDTH), F32)
        cn_ref[...] = jnp.zeros(cn_ref.shape, F32)
        m_ref[...] = jnp.zeros(m_ref.shape, F32)

    @pl.when(step > 0)
    def _():
        ubuf[0:halo, :] = ubuf[tb:tb + halo, :]

    ubuf[halo:halo + tb, :] = qk_ref[...]
    w = convw_ref[...]
    y = ubuf[halo:halo + tb, :] * w[0:1, :]
    for j in range(1, CONV_WIDTH):
        y = y + ubuf[halo - j:halo - j + tb, :] * w[j:j + 1, :]
    qkc[...] = y * jax.nn.sigmoid(y)

    tri = (lax.broadcasted_iota(jnp.int32, (L, L), 0) >= lax.broadcasted_iota(jnp.int32, (L, L), 1))
    ones_col = (lax.broadcasted_iota(jnp.int32, (L, HEAD_DIM), 1) == 0).astype(F32)
    q_scale = HEAD_DIM ** -0.5

    for hd in range(MLSTM_HEADS):
        hs = slice(hd * HEAD_DIM, (hd + 1) * HEAD_DIM)
        ks = slice(MLSTM_WIDTH + hd * HEAD_DIM, MLSTM_WIDTH + (hd + 1) * HEAD_DIM)
        for c in range(tb // L):
            rows = slice(c * L, (c + 1) * L)
            q = (qkc[rows, hs] * q_scale).astype(BF16)
            k = qkc[rows, ks]
            k_bf = k.astype(BF16)
            vaug = jnp.concatenate([v_ref[rows, hs], ones_col], axis=1)
            i_col = gcol_ref[rows, hd:hd + 1]
            b_col = gcol_ref[rows, MLSTM_HEADS + hd:MLSTM_HEADS + hd + 1]
            i_row = grow_ref[0, hd:hd + 1, rows]
            b_row = grow_ref[0, MLSTM_HEADS + hd:MLSTM_HEADS + hd + 1, rows]
            m_prev = m_ref[hd][0:1, 0:1]
            cn = cn_ref[hd]

            log_d = jnp.where(tri, b_col - b_row + i_row, -jnp.inf)
            inter = b_col + m_prev
            m_t = jnp.maximum(inter, jnp.max(log_d, axis=-1, keepdims=True))
            s = lax.dot_general(q, k_bf, (((1,), (1,)), ((), ())), preferred_element_type=F32)
            scores = (s * jnp.exp(log_d - m_t)).astype(BF16)
            inter_w = jnp.exp(inter - m_t)
            numden = (jnp.dot(scores, vaug.astype(BF16), preferred_element_type=F32)
                      + inter_w * jnp.dot(q, cn.astype(BF16), preferred_element_type=F32))
            num = numden[:, 0:HEAD_DIM]
            den = numden[:, HEAD_DIM:HEAD_DIM + 1]
            h = num / jnp.maximum(jnp.abs(den), jnp.exp(-m_t))

            b_last = b_row[:, L - 1:L]
            log_w = b_last - b_col + i_col
            m_new = jnp.maximum(b_last + m_prev, jnp.max(log_w, axis=0, keepdims=True))
            wv = (jnp.exp(log_w - m_new) * vaug).astype(BF16)
            decay = jnp.exp(b_last + m_prev - m_new)
            cn_ref[hd] = decay * cn + jnp.dot(k.T.astype(BF16), wv, preferred_element_type=F32)
            m_ref[hd] = jnp.broadcast_to(m_new, (SUBLANES, LANES))

            hn = h * _rms_scale(h)
            out_ref[rows, hs] = (hn * hg_ref[:, hs] * jax.nn.sigmoid(o_ref[rows, hs])).astype(BF16)


def _mlstm(pm, gcol, grow, conv_w, head_g, *, batch, seq, tb):
    t = pm.shape[0]
    assert seq % tb == 0 and tb % MLSTM_CHUNK == 0
    nb = seq // tb
    w2 = 2 * MLSTM_WIDTH
    blocks = (_nbytes((tb, w2), F32) + 2 * _nbytes((tb, MLSTM_WIDTH), F32) + _nbytes((tb, LANES), F32)
              + _nbytes((N_GATES, tb), F32) + _nbytes((tb, MLSTM_WIDTH), BF16))
    scratch = (_nbytes((tb + 2 * SUBLANES, w2), F32) + _nbytes((tb, w2), F32)
               + _nbytes((MLSTM_HEADS, HEAD_DIM, 2 * HEAD_DIM), F32))
    temps = 2 * _nbytes((tb, w2), F32)
    return pl.pallas_call(
        functools.partial(_mlstm_kernel, tb=tb),
        grid=(batch, nb),
        in_specs=[
            pl.BlockSpec((tb, w2), lambda b, i: (b * nb + i, 0)),
            pl.BlockSpec((tb, MLSTM_WIDTH), lambda b, i: (b * nb + i, 2)),
            pl.BlockSpec((tb, MLSTM_WIDTH), lambda b, i: (b * nb + i, 3)),
            pl.BlockSpec((tb, LANES), lambda b, i: (b * nb + i, 0)),
            pl.BlockSpec((1, N_GATES, tb), lambda b, i: (b, 0, i)),
            pl.BlockSpec((CONV_WIDTH, w2), lambda b, i: (0, 0)),
            pl.BlockSpec((1, MLSTM_WIDTH), lambda b, i: (0, 0)),
        ],
        out_specs=pl.BlockSpec((tb, MLSTM_WIDTH), lambda b, i: (b * nb + i, 0)),
        out_shape=jax.ShapeDtypeStruct((t, MLSTM_WIDTH), BF16),
        scratch_shapes=[
            pltpu.VMEM((tb + 2 * SUBLANES, w2), F32),
            pltpu.VMEM((tb, w2), F32),
            pltpu.VMEM((MLSTM_HEADS, HEAD_DIM, 2 * HEAD_DIM), F32),
            pltpu.VMEM((MLSTM_HEADS, SUBLANES, LANES), F32),
        ],
        compiler_params=pltpu.CompilerParams(
            dimension_semantics=("parallel", "arbitrary"),
            vmem_limit_bytes=_vmem_limit(blocks, scratch, temps)),
        name="mlstm",
    )(pm, pm, pm, gcol, grow, conv_w, head_g.reshape(1, MLSTM_WIDTH))


def _pool_kernel(u_ref, w_ref, scale_ref, out_ref, ext, *, ts):
    step = pl.program_id(1)
    halo = 2 * SUBLANES
    assert halo >= max(POOL_WINDOWS)

    @pl.when(step == 0)
    def _():
        ext[0:halo, :] = jnp.zeros((halo, POOL_WIDTH), F32)

    @pl.when(step > 0)
    def _():
        ext[0:halo, :] = ext[ts:ts + halo, :]

    ext[halo:halo + ts, :] = u_ref[...]
    pos = (step * ts + lax.broadcasted_iota(jnp.int32, (ts, 1), 0) + 1).astype(F32)
    for g, win in enumerate(POOL_WINDOWS):
        cols = slice(g * POOL_GROUP_DIM, (g + 1) * POOL_GROUP_DIM)
        acc = ext[halo:halo + ts, cols]
        for j in range(1, win):
            acc = acc + ext[halo - j:halo - j + ts, cols]
        mean = acc / jnp.minimum(pos, float(win))
        diff = (mean - ext[halo:halo + ts, cols]).astype(BF16)
        y = jnp.dot(diff, w_ref[g], preferred_element_type=F32)
        out_ref[:, cols] = (y * scale_ref[:, cols]).astype(BF16)


def _pool(pm, pool_w, pool_scale, *, batch, seq, ts):
    t = pm.shape[0]
    assert seq % ts == 0
    nb = seq // ts
    col_block = (4 * MLSTM_WIDTH) // POOL_WIDTH
    blocks = (_nbytes((ts, POOL_WIDTH), F32) + _nbytes(pool_w.shape, BF16)
              + _nbytes((ts, POOL_WIDTH), BF16))
    scratch = _nbytes((ts + 2 * SUBLANES, POOL_WIDTH), F32)
    temps = 4 * _nbytes((ts, POOL_GROUP_DIM), F32)
    return pl.pallas_call(
        functools.partial(_pool_kernel, ts=ts),
        grid=(batch, nb),
        in_specs=[
            pl.BlockSpec((ts, POOL_WIDTH), lambda b, i: (b * nb + i, col_block)),
            pl.BlockSpec(pool_w.shape, lambda b, i: (0, 0, 0)),
            pl.BlockSpec((1, POOL_WIDTH), lambda b, i: (0, 0)),
        ],
        out_specs=pl.BlockSpec((ts, POOL_WIDTH), lambda b, i: (b * nb + i, 0)),
        out_shape=jax.ShapeDtypeStruct((t, POOL_WIDTH), BF16),
        scratch_shapes=[pltpu.VMEM((ts + 2 * SUBLANES, POOL_WIDTH), F32)],
        compiler_params=pltpu.CompilerParams(
            dimension_semantics=("parallel", "arbitrary"),
            vmem_limit_bytes=_vmem_limit(blocks, scratch, temps)),
        name="pool",
    )(pm, pool_w, pool_scale.reshape(1, POOL_WIDTH))


def _split3_bf16(c):
    hi = c.astype(BF16).astype(F32)
    r = c - hi
    mid = r.astype(BF16).astype(F32)
    lo = (r - mid).astype(BF16).astype(F32)
    return hi, mid, lo


def _bias_rows(c_row, n, *, query_side):
    hi, mid, lo = _split3_bf16(c_row)
    ridx = lax.broadcasted_iota(jnp.int32, (HEAD_DIM, n), 0)
    sign = 1.0 if query_side else -1.0
    first = 0 if query_side else 3
    ones_at = 3 if query_side else 0
    rows = jnp.where(ridx == first, sign * hi, 0.0)
    rows = jnp.where(ridx == first + 1, sign * mid, rows)
    rows = jnp.where(ridx == first + 2, sign * lo, rows)
    return jnp.where((ridx >= ones_at) & (ridx < ones_at + 3), 1.0, rows)


def _fox_kernel(q_ref, k_ref, v_ref, c_ref, out_ref, kt_ref, va_ref, qa_ref, m_ref, acc_ref,
                *, tq, nq, n_parts):
    qi = pl.program_id(2)

    @pl.when(qi == 0)
    def _():
        ones_col = (lax.broadcasted_iota(jnp.int32, (tq, HEAD_DIM), 1) == 0).astype(BF16)
        for jb in range(nq):
            rows = slice(jb * tq, (jb + 1) * tq)
            kt = k_ref[rows, :].astype(F32).T
            ext = _bias_rows(c_ref[0, jb:jb + 1, :] * LOG2E, tq, query_side=False)
            kt_ref[jb] = jnp.concatenate([kt, ext], axis=0).astype(BF16)
            va_ref[rows, 0:HEAD_DIM] = v_ref[rows, :]
            va_ref[rows, HEAD_DIM:] = ones_col

    c2_q = c_ref[0, pl.ds(qi, 1), :] * LOG2E
    qa_ref[:, 0:HEAD_DIM] = q_ref[...]
    qa_ref[:, HEAD_DIM:] = _bias_rows(c2_q, tq, query_side=True).T.astype(BF16)
    m_ref[...] = jnp.full(m_ref.shape, -jnp.inf, F32)
    acc_ref[...] = jnp.zeros(acc_ref.shape, F32)

    def logits(r0, nr, kv_blk, nc, masked):
        s = jnp.dot(qa_ref[r0:r0 + nr, :], kt_ref[kv_blk][:, 0:nc], preferred_element_type=F32)
        if masked:
            keep = (lax.broadcasted_iota(jnp.int32, (nr, nc), 1)
                    <= lax.broadcasted_iota(jnp.int32, (nr, nc), 0) + r0)
            s = jnp.where(keep, s, -jnp.inf)
        return s

    def accumulate(s, r0, nr, kv_blk, nc):
        m_prev = m_ref[r0:r0 + nr, :]
        m_new = jnp.maximum(m_prev, jnp.max(s, axis=-1, keepdims=True))
        alpha = jnp.exp2(m_prev - m_new)
        p = jnp.exp2(s - m_new).astype(BF16)
        v_rows = pl.ds(pl.multiple_of(kv_blk * tq, tq), nc)
        acc_ref[r0:r0 + nr, :] = (alpha * acc_ref[r0:r0 + nr, :]
                                  + jnp.dot(p, va_ref[v_rows, :], preferred_element_type=F32))
        m_ref[r0:r0 + nr, :] = m_new

    def update(parts, kv_blk, masked):
        ss = [logits(r0, nr, kv_blk, nc, masked) for r0, nr, nc in parts]
        for s, (r0, nr, nc) in zip(ss, parts):
            accumulate(s, r0, nr, kv_blk, nc)

    pr = tq // n_parts

    def body(j, carry):
        update([(i * pr, pr, tq) for i in range(n_parts)], j, False)
        return carry

    lax.fori_loop(0, qi, body, 0)
    update([(i * pr, pr, (i + 1) * pr) for i in range(n_parts)], qi, True)
    out_ref[...] = (acc_ref[:, 0:HEAD_DIM] / acc_ref[:, HEAD_DIM:HEAD_DIM + 1]).astype(BF16)


def _fox(pa, grow, *, batch, seq, tq, n_parts):
    t = pa.shape[0]
    assert seq % tq == 0
    nq = seq // tq
    assert tq % n_parts == 0 and (tq // n_parts) % LANES == 0
    c = grow.reshape(batch * N_GATES, nq, tq)
    gate0 = 2 * MLSTM_HEADS
    blocks = (2 * _nbytes((tq, HEAD_DIM), BF16) + 2 * _nbytes((seq, HEAD_DIM), BF16)
              + _nbytes((max(nq, SUBLANES), tq), F32))
    scratch = (2 * _nbytes((seq, 2 * HEAD_DIM), BF16) + _nbytes((tq, 2 * HEAD_DIM), BF16)
               + _nbytes((tq, LANES), F32) + _nbytes((tq, 2 * HEAD_DIM), F32))
    temps = 3 * _nbytes((tq, tq), F32)
    return pl.pallas_call(
        functools.partial(_fox_kernel, tq=tq, nq=nq, n_parts=n_parts),
        grid=(batch, FOX_HEADS, nq),
        in_specs=[
            pl.BlockSpec((tq, HEAD_DIM), lambda b, h, i: (b * nq + i, h)),
            pl.BlockSpec((seq, HEAD_DIM), lambda b, h, i: (b, FOX_HEADS + h)),
            pl.BlockSpec((seq, HEAD_DIM), lambda b, h, i: (b, 2 * FOX_HEADS + h)),
            pl.BlockSpec((1, nq, tq), lambda b, h, i: (b * N_GATES + gate0 + h, 0, 0)),
        ],
        out_specs=pl.BlockSpec((tq, HEAD_DIM), lambda b, h, i: (b * nq + i, h)),
        out_shape=jax.ShapeDtypeStruct((t, FOX_WIDTH), BF16),
        scratch_shapes=[
            pltpu.VMEM((nq, 2 * HEAD_DIM, tq), BF16),
            pltpu.VMEM((seq, 2 * HEAD_DIM), BF16),
            pltpu.VMEM((tq, 2 * HEAD_DIM), BF16),
            pltpu.VMEM((tq, 1), F32),
            pltpu.VMEM((tq, 2 * HEAD_DIM), F32),
        ],
        compiler_params=pltpu.CompilerParams(
            dimension_semantics=("parallel", "parallel", "arbitrary"),
            vmem_limit_bytes=_vmem_limit(blocks, scratch, temps)),
        name="fox",
    )(pa, pa, pa, c)


def _proj_out_kernel(hm_ref, hp_ref, ha_ref, w_ref, x_ref, g_ref, o_ref, *, row_chunk):
    tm = x_ref.shape[0]
    y = jnp.dot(hm_ref[...], w_ref[0:MLSTM_WIDTH, :], preferred_element_type=F32)
    y = y + jnp.dot(hp_ref[...], w_ref[MLSTM_WIDTH:MLSTM_WIDTH + POOL_WIDTH, :],
                    preferred_element_type=F32)
    y = y + jnp.dot(ha_ref[...], w_ref[MLSTM_WIDTH + POOL_WIDTH:, :], preferred_element_type=F32)
    o_ref[...] = y

    def body(r, c):
        rows = pl.ds(pl.multiple_of(r * row_chunk, row_chunk), row_chunk)
        yr = o_ref[rows, :]
        o_ref[rows, :] = x_ref[rows, :] + yr * _rms_scale(yr) * g_ref[...]
        return c
    lax.fori_loop(0, tm // row_chunk, body, 0)


def _proj_out(hm, hp, ha, w_out, x, g, *, tm):
    t, d = x.shape
    assert t % tm == 0
    blocks = (_nbytes((tm, d), BF16) + _nbytes((d, d), BF16) + 2 * _nbytes((tm, d), F32))
    temps = 2 * _nbytes((tm, d), F32)
    return pl.pallas_call(
        functools.partial(_proj_out_kernel, row_chunk=64),
        grid=(t // tm,),
        in_specs=[
            pl.BlockSpec((tm, MLSTM_WIDTH), lambda i: (i, 0)),
            pl.BlockSpec((tm, POOL_WIDTH), lambda i: (i, 0)),
            pl.BlockSpec((tm, FOX_WIDTH), lambda i: (i, 0)),
            pl.BlockSpec((d, d), lambda i: (0, 0)),
            pl.BlockSpec((tm, d), lambda i: (i, 0)),
            pl.BlockSpec((1, d), lambda i: (0, 0)),
        ],
        out_specs=pl.BlockSpec((tm, d), lambda i: (i, 0)),
        out_shape=jax.ShapeDtypeStruct((t, d), F32),
        compiler_params=pltpu.CompilerParams(
            dimension_semantics=("parallel",),
            vmem_limit_bytes=_vmem_limit(blocks, 0, temps)),
        name="proj_out",
    )(hm, hp, ha, w_out, x, g.reshape(1, d))


def _mixer(x, pre_g, post_g, w_in, conv_w, b_i, b_f, head_g, pool_w, pool_scale, fox_b_f, w_out,
           *, batch, seq):
    w_pm = jnp.concatenate([w_in[:, :OFF_MI], w_in[:, OFF_POOL:OFF_AQ]], axis=1).astype(BF16)
    w_pa = w_in[:, OFF_AQ:OFF_AF].astype(BF16)
    w_gate = jnp.concatenate([w_in[:, OFF_MI:OFF_POOL], w_in[:, OFF_AF:N_IN]], axis=1)
    w_gate = jnp.pad(w_gate, ((0, 0), (0, LANES - N_GATES))).astype(BF16)
    bias = jnp.concatenate([b_i, b_f, fox_b_f]).reshape(N_GATES, 1)

    pm, pa, pg = _proj_in(x, pre_g, w_pm, w_pa, w_gate, tm=256)
    grow, gcol = _gates(pg, bias, batch=batch, seq=seq)
    hm = _mlstm(pm, gcol, grow, conv_w, head_g, batch=batch, seq=seq, tb=min(512, seq))
    hp = _pool(pm, pool_w.astype(BF16), pool_scale, batch=batch, seq=seq, ts=min(512, seq))
    ha = _fox(pa, grow, batch=batch, seq=seq, tq=min(1024, seq), n_parts=4)
    return _proj_out(hm, hp, ha, w_out.astype(BF16), x, post_g, tm=512)


def kernel(x, ffn1_pre_g, ffn1_post_g, ffn1_w_gate, ffn1_w_up, ffn1_w_down, mix_pre_g, mix_post_g,
           w_in, mlstm_conv, mlstm_b_i, mlstm_b_f, mlstm_head_g, pool_w, pool_scale, fox_b_f, w_out,
           ffn2_pre_g, ffn2_post_g, ffn2_w_gate, ffn2_w_up, ffn2_w_down):
    batch, seq, d = x.shape
    depth = w_in.shape[0]
    xt = x.reshape(batch * seq, d)
    for l in range(depth):
        xt = _ffn(xt, ffn1_pre_g[l], ffn1_post_g[l], ffn1_w_gate[l].astype(BF16),
                  ffn1_w_up[l].astype(BF16), ffn1_w_down[l].astype(BF16), tm=512, tf=512)
        xt = _mixer(xt, mix_pre_g[l], mix_post_g[l], w_in[l], mlstm_conv[l], mlstm_b_i[l],
                    mlstm_b_f[l], mlstm_head_g[l], pool_w[l], pool_scale[l], fox_b_f[l], w_out[l],
                    batch=batch, seq=seq)
        xt = _ffn(xt, ffn2_pre_g[l], ffn2_post_g[l], ffn2_w_gate[l].astype(BF16),
                  ffn2_w_up[l].astype(BF16), ffn2_w_down[l].astype(BF16), tm=512, tf=512)
    return xt.reshape(batch, seq, d)
```

```python
import functools

import jax
import jax.numpy as jnp
from jax import lax
from jax.experimental import pallas as pl
from jax.experimental.pallas import tpu as pltpu

F32 = jnp.float32
BF16 = jnp.bfloat16

D_MODEL = 2048
HEAD_DIM = 128
MLSTM_WIDTH = D_MODEL // 4
MLSTM_HEADS = MLSTM_WIDTH // HEAD_DIM
POOL_WIDTH = D_MODEL // 4
POOL_WINDOWS = (2, 4, 8, 16)
POOL_GROUP_DIM = POOL_WIDTH // len(POOL_WINDOWS)
FOX_WIDTH = D_MODEL - MLSTM_WIDTH - POOL_WIDTH
FOX_HEADS = FOX_WIDTH // HEAD_DIM
CONV_WIDTH = 4
MLSTM_CHUNK = 128
RMS_EPS = 1e-6
FFN_RESIDUAL_WEIGHT = 0.5

OFF_MI = 4 * MLSTM_WIDTH
OFF_MF = OFF_MI + MLSTM_HEADS
OFF_POOL = OFF_MF + MLSTM_HEADS
OFF_AQ = OFF_POOL + POOL_WIDTH
OFF_AF = OFF_AQ + 3 * FOX_WIDTH
N_IN = OFF_AF + FOX_HEADS

PM_WIDTH = 4 * MLSTM_WIDTH + POOL_WIDTH
PA_WIDTH = 3 * FOX_WIDTH
N_GATES = 2 * MLSTM_HEADS + FOX_HEADS
LOG2E = 1.4426950408889634
FOX_Q_SCALE = LOG2E * HEAD_DIM ** -0.5
LANES = 128
SUBLANES = 8
VMEM_LIMIT_CAP = 58 * 1024 * 1024


def _vmem_limit(block_bytes, scratch_bytes, temp_bytes):
    return int(min(2 * block_bytes + scratch_bytes + temp_bytes + (4 << 20), VMEM_LIMIT_CAP))


def _nbytes(shape, dtype):
    n = 1
    for s in shape:
        n *= s
    return n * jnp.dtype(dtype).itemsize


def _rms_scale(x):
    return lax.rsqrt(jnp.mean(x * x, axis=-1, keepdims=True) + RMS_EPS)


CAST_BLOCK_BYTES = 6 << 20


def _cast_kernel(w_ref, o_ref):
    o_ref[...] = w_ref[...].astype(BF16)


def _layer_bf16(w, layer):
    _, r, c = w.shape
    tr = r
    while _nbytes((tr, c), F32) > CAST_BLOCK_BYTES and tr % 2 == 0 and (tr // 2) % 16 == 0:
        tr //= 2
    assert r % tr == 0
    blocks = _nbytes((tr, c), F32) + _nbytes((tr, c), BF16)
    return pl.pallas_call(
        _cast_kernel,
        grid=(r // tr,),
        in_specs=[pl.BlockSpec((None, tr, c), lambda i: (layer, i, 0))],
        out_specs=pl.BlockSpec((tr, c), lambda i: (i, 0)),
        out_shape=jax.ShapeDtypeStruct((r, c), BF16),
        compiler_params=pltpu.CompilerParams(
            dimension_semantics=("parallel",),
            vmem_limit_bytes=_vmem_limit(blocks, 0, 0)),
        name="cast",
    )(w)


def _split_w_in_kernel(w_ref, pm_ref, pa_ref, gate_ref):
    pm_ref[:, 0:OFF_MI] = w_ref[:, 0:OFF_MI].astype(BF16)
    pm_ref[:, OFF_MI:PM_WIDTH] = w_ref[:, OFF_POOL:OFF_AQ].astype(BF16)
    pa_ref[...] = w_ref[:, OFF_AQ:OFF_AF].astype(BF16)
    gates = jnp.concatenate([w_ref[:, OFF_MI:OFF_POOL], w_ref[:, OFF_AF:N_IN],
                             jnp.zeros((w_ref.shape[0], LANES - N_GATES), F32)], axis=1)
    gate_ref[...] = gates.astype(BF16)


def _split_w_in(w_in, layer, *, tr):
    _, r, c = w_in.shape
    assert r % tr == 0 and c == N_IN
    blocks = _nbytes((tr, c), F32) + _nbytes((tr, PM_WIDTH + PA_WIDTH + LANES), BF16)
    return pl.pallas_call(
        _split_w_in_kernel,
        grid=(r // tr,),
        in_specs=[pl.BlockSpec((None, tr, c), lambda i: (layer, i, 0))],
        out_specs=[
            pl.BlockSpec((tr, PM_WIDTH), lambda i: (i, 0)),
            pl.BlockSpec((tr, PA_WIDTH), lambda i: (i, 0)),
            pl.BlockSpec((tr, LANES), lambda i: (i, 0)),
        ],
        out_shape=[
            jax.ShapeDtypeStruct((r, PM_WIDTH), BF16),
            jax.ShapeDtypeStruct((r, PA_WIDTH), BF16),
            jax.ShapeDtypeStruct((r, LANES), BF16),
        ],
        compiler_params=pltpu.CompilerParams(
            dimension_semantics=("parallel",),
            vmem_limit_bytes=_vmem_limit(blocks, 0, 2 * _nbytes((tr, c), F32))),
        name="split_w_in",
    )(w_in)


NORM_ROWS = 32
NORM_COLS = 512


def _row_rms_scale(ref, rows):
    d = ref.shape[1]
    ss = None
    for c0 in range(0, d, NORM_COLS):
        v = ref[rows, c0:c0 + NORM_COLS]
        ss = v * v if ss is None else ss + v * v
    return lax.rsqrt(jnp.sum(ss, axis=-1, keepdims=True) * (1.0 / d) + RMS_EPS)


def _normalise_rows(n_rows, src_ref, apply_fn):
    groups = 2
    trip_rows = groups * NORM_ROWS
    n_trips = n_rows // trip_rows

    def rows_of(r):
        base = pl.multiple_of(r * trip_rows, trip_rows)
        return [pl.ds(base + k * NORM_ROWS, NORM_ROWS) for k in range(groups)]

    def scales_of(r):
        return tuple(_row_rms_scale(src_ref, rows) for rows in rows_of(r))

    def body(r, scales):
        nxt = scales_of(jnp.minimum(r + 1, n_trips - 1))
        for rows, scale in zip(rows_of(r), scales):
            apply_fn(rows, scale)
        return nxt

    lax.fori_loop(0, n_trips, body, scales_of(0))


def _prenorm(x_ref, g_ref, h_ref):
    def apply(rows, scale):
        for c0 in range(0, x_ref.shape[1], NORM_COLS):
            cols = slice(c0, c0 + NORM_COLS)
            h_ref[rows, cols] = (x_ref[rows, cols] * scale * g_ref[:, cols]).astype(BF16)
    _normalise_rows(x_ref.shape[0], x_ref, apply)


def _postnorm_residual(x_ref, y_ref, g_ref, o_ref, weight):
    def apply(rows, scale):
        scale = scale * weight
        for c0 in range(0, x_ref.shape[1], NORM_COLS):
            cols = slice(c0, c0 + NORM_COLS)
            o_ref[rows, cols] = x_ref[rows, cols] + y_ref[rows, cols] * scale * g_ref[:, cols]
    _normalise_rows(x_ref.shape[0], y_ref, apply)


def _ffn_kernel(x_ref, pre_g_ref, post_g_ref, wg_ref, wu_ref, wd_ref, o_ref, h_ref, *, n_f):
    j = pl.program_id(1)
    tm = x_ref.shape[0]

    @pl.when(j == 0)
    def _():
        _prenorm(x_ref, pre_g_ref, h_ref)
        o_ref[...] = jnp.zeros(o_ref.shape, F32)

    h = h_ref[...]
    g = jnp.dot(h, wg_ref[...], preferred_element_type=F32)
    u = jnp.dot(h, wu_ref[...], preferred_element_type=F32)
    a = (g * jax.nn.sigmoid(g) * u).astype(BF16)
    o_ref[...] += jnp.dot(a, wd_ref[...], preferred_element_type=F32)

    @pl.when(j == n_f - 1)
    def _():
        _postnorm_residual(x_ref, o_ref, post_g_ref, o_ref, FFN_RESIDUAL_WEIGHT)


def _ffn(x, pre_g, post_g, wg, wu, wd, *, tm, tf):
    t, d = x.shape
    f = wg.shape[1]
    assert t % tm == 0 and f % tf == 0 and FFN_RESIDUAL_WEIGHT == 0.5
    blocks = (_nbytes((tm, d), F32) * 2 + 2 * _nbytes((d, tf), BF16) + _nbytes((tf, d), BF16))
    scratch = _nbytes((tm, d), BF16)
    temps = 3 * _nbytes((tm, tf), F32)
    return pl.pallas_call(
        functools.partial(_ffn_kernel, n_f=f // tf),
        grid=(t // tm, f // tf),
        in_specs=[
            pl.BlockSpec((tm, d), lambda i, j: (i, 0)),
            pl.BlockSpec((1, d), lambda i, j: (0, 0)),
            pl.BlockSpec((1, d), lambda i, j: (0, 0)),
            pl.BlockSpec((d, tf), lambda i, j: (0, j)),
            pl.BlockSpec((d, tf), lambda i, j: (0, j)),
            pl.BlockSpec((tf, d), lambda i, j: (j, 0)),
        ],
        out_specs=pl.BlockSpec((tm, d), lambda i, j: (i, 0)),
        out_shape=jax.ShapeDtypeStruct((t, d), F32),
        scratch_shapes=[pltpu.VMEM((tm, d), BF16)],
        compiler_params=pltpu.CompilerParams(
            dimension_semantics=("parallel", "arbitrary"),
            vmem_limit_bytes=_vmem_limit(blocks, scratch, temps)),
        name="ffn",
    )(x, pre_g.reshape(1, d), post_g.reshape(1, d), wg, wu, wd)


def _proj_in_kernel(x_ref, g_ref, wpm_ref, wpa_ref, wgate_ref, pm_ref, pa_ref, pg_ref):
    x = x_ref[...]
    h = (x * _rms_scale(x) * g_ref[...]).astype(BF16)
    pm_ref[...] = jnp.dot(h, wpm_ref[...], preferred_element_type=F32)
    pa = jnp.dot(h, wpa_ref[...], preferred_element_type=F32)
    pa_ref[:, 0:FOX_WIDTH] = (pa[:, 0:FOX_WIDTH] * FOX_Q_SCALE).astype(BF16)
    pa_ref[:, FOX_WIDTH:] = pa[:, FOX_WIDTH:].astype(BF16)
    pg_ref[...] = jnp.dot(h, wgate_ref[...], preferred_element_type=F32)


def _proj_in(x, g, w_pm, w_pa, w_gate, *, tm):
    t, d = x.shape
    assert t % tm == 0 and w_pm.shape == (d, PM_WIDTH) and w_pa.shape == (d, PA_WIDTH)
    resident = pl.Buffered(1)
    blocks = (_nbytes((tm, d), F32) + _nbytes((d, LANES), BF16) + _nbytes((tm, PM_WIDTH), F32)
              + _nbytes((tm, PA_WIDTH), BF16) + _nbytes((tm, LANES), F32))
    scratch = _nbytes((d, PM_WIDTH + PA_WIDTH), BF16)
    temps = _nbytes((tm, d), BF16) + 2 * _nbytes((tm, PA_WIDTH), F32)
    return pl.pallas_call(
        _proj_in_kernel,
        grid=(t // tm,),
        in_specs=[
            pl.BlockSpec((tm, d), lambda i: (i, 0)),
            pl.BlockSpec((1, d), lambda i: (0, 0)),
            pl.BlockSpec((d, PM_WIDTH), lambda i: (0, 0), pipeline_mode=resident),
            pl.BlockSpec((d, PA_WIDTH), lambda i: (0, 0), pipeline_mode=resident),
            pl.BlockSpec((d, LANES), lambda i: (0, 0)),
        ],
        out_specs=[
            pl.BlockSpec((tm, PM_WIDTH), lambda i: (i, 0)),
            pl.BlockSpec((tm, PA_WIDTH), lambda i: (i, 0)),
            pl.BlockSpec((tm, LANES), lambda i: (i, 0)),
        ],
        out_shape=[
            jax.ShapeDtypeStruct((t, PM_WIDTH), F32),
            jax.ShapeDtypeStruct((t, PA_WIDTH), BF16),
            jax.ShapeDtypeStruct((t, LANES), F32),
        ],
        compiler_params=pltpu.CompilerParams(
            dimension_semantics=("parallel",),
            vmem_limit_bytes=_vmem_limit(blocks, scratch, temps)),
        name="proj_in",
    )(x, g.reshape(1, d), w_pm, w_pa, w_gate)


def _log_sigmoid(z):
    return -(jnp.maximum(-z, 0.0) + jnp.log1p(jnp.exp(-jnp.abs(z))))


def _lane_scan(v, shifts, lane_pos):
    for sh in shifts:
        v = v + jnp.where(lane_pos >= sh, pltpu.roll(v, sh, axis=1), 0.0)
    return v


def _gates_kernel(pg_ref, bias_ref, row_ref, col_ref):
    s = pg_ref.shape[0]
    z = pg_ref[...].T[0:N_GATES, :] + bias_ref[...]
    lane = lax.broadcasted_iota(jnp.int32, (1, s), 1)
    mi = z[0:MLSTM_HEADS]
    lf_m = _log_sigmoid(z[MLSTM_HEADS:2 * MLSTM_HEADS])
    lf_a = _log_sigmoid(z[2 * MLSTM_HEADS:N_GATES])
    chunk_shifts = [1 << k for k in range(MLSTM_CHUNK.bit_length() - 1)]
    seq_shifts = [1 << k for k in range((s - 1).bit_length())]
    bcum = _lane_scan(lf_m, chunk_shifts, lane % MLSTM_CHUNK)
    cfox = _lane_scan(lf_a, seq_shifts, lane)
    rows = jnp.concatenate([mi, bcum, cfox], axis=0)
    row_ref[0] = rows
    padded = jnp.concatenate([rows, jnp.zeros((LANES - N_GATES, s), F32)], axis=0)
    col_ref[...] = padded.T


def _gates(pg, bias, *, batch, seq):
    t = pg.shape[0]
    assert t == batch * seq
    blocks = 2 * _nbytes((seq, LANES), F32) + _nbytes((N_GATES, seq), F32)
    temps = 6 * _nbytes((seq, LANES), F32)
    return pl.pallas_call(
        _gates_kernel,
        grid=(batch,),
        in_specs=[
            pl.BlockSpec((seq, LANES), lambda b: (b, 0)),
            pl.BlockSpec((N_GATES, 1), lambda b: (0, 0)),
        ],
        out_specs=[
            pl.BlockSpec((1, N_GATES, seq), lambda b: (b, 0, 0)),
            pl.BlockSpec((seq, LANES), lambda b: (b, 0)),
        ],
        out_shape=[
            jax.ShapeDtypeStruct((batch, N_GATES, seq), F32),
            jax.ShapeDtypeStruct((t, LANES), F32),
        ],
        compiler_params=pltpu.CompilerParams(
            dimension_semantics=("parallel",),
            vmem_limit_bytes=_vmem_limit(blocks, 0, temps)),
        name="gates",
    )(pg, bias)


def _mlstm_kernel(qk_ref, v_ref, o_ref, gcol_ref, grow_ref, convw_ref, hg_ref, out_ref,
                  ubuf, qkc, cn_ref, m_ref, *, tb):
    step = pl.program_id(1)
    L = MLSTM_CHUNK
    halo = SUBLANES

    @pl.when(step == 0)
    def _():
        ubuf[0:halo, :] = jnp.zeros((halo, 2 * MLSTM_WIDTH), F32)
        cn_ref[...] = jnp.zeros(cn_ref.shape, F32)
        m_ref[...] = jnp.zeros(m_ref.shape, F32)

    @pl.when(step > 0)
    def _():
        ubuf[0:halo, :] = ubuf[tb:tb + halo, :]

    ubuf[halo:halo + tb, :] = qk_ref[...]
    w = convw_ref[...]
    y = ubuf[halo:halo + tb, :] * w[0:1, :]
    for j in range(1, CONV_WIDTH):
        y = y + ubuf[halo - j:halo - j + tb, :] * w[j:j + 1, :]
    qkc[...] = y * jax.nn.sigmoid(y)

    tri = (lax.broadcasted_iota(jnp.int32, (L, L), 0) >= lax.broadcasted_iota(jnp.int32, (L, L), 1))
    ones_col = (lax.broadcasted_iota(jnp.int32, (L, HEAD_DIM), 1) == 0).astype(F32)
    q_scale = HEAD_DIM ** -0.5

    for hd in range(MLSTM_HEADS):
        hs = slice(hd * HEAD_DIM, (hd + 1) * HEAD_DIM)
        ks = slice(MLSTM_WIDTH + hd * HEAD_DIM, MLSTM_WIDTH + (hd + 1) * HEAD_DIM)
        for c in range(tb // L):
            rows = slice(c * L, (c + 1) * L)
            q = (qkc[rows, hs] * q_scale).astype(BF16)
            k = qkc[rows, ks]
            k_bf = k.astype(BF16)
            vaug = jnp.concatenate([v_ref[rows, hs], ones_col], axis=1)
            i_col = gcol_ref[rows, hd:hd + 1]
            b_col = gcol_ref[rows, MLSTM_HEADS + hd:MLSTM_HEADS + hd + 1]
            i_row = grow_ref[0, hd:hd + 1, rows]
            b_row = grow_ref[0, MLSTM_HEADS + hd:MLSTM_HEADS + hd + 1, rows]
            m_prev = m_ref[hd][0:1, 0:1]
            cn = cn_ref[hd]

            log_d = jnp.where(tri, b_col - b_row + i_row, -jnp.inf)
            inter = b_col + m_prev
            m_t = jnp.maximum(inter, jnp.max(log_d, axis=-1, keepdims=True))
            s = lax.dot_general(q, k_bf, (((1,), (1,)), ((), ())), preferred_element_type=F32)
            scores = (s * jnp.exp(log_d - m_t)).astype(BF16)
            inter_w = jnp.exp(inter - m_t)
            numden = (jnp.dot(scores, vaug.astype(BF16), preferred_element_type=F32)
                      + inter_w * jnp.dot(q, cn.astype(BF16), preferred_element_type=F32))
            num = numden[:, 0:HEAD_DIM]
            den = numden[:, HEAD_DIM:HEAD_DIM + 1]
            h = num / jnp.maximum(jnp.abs(den), jnp.exp(-m_t))

            b_last = b_row[:, L - 1:L]
            log_w = b_last - b_col + i_col
            m_new = jnp.maximum(b_last + m_prev, jnp.max(log_w, axis=0, keepdims=True))
            wv = (jnp.exp(log_w - m_new) * vaug).astype(BF16)
            decay = jnp.exp(b_last + m_prev - m_new)
            cn_ref[hd] = decay * cn + jnp.dot(k.T.astype(BF16), wv, preferred_element_type=F32)
            m_ref[hd] = jnp.broadcast_to(m_new, (SUBLANES, LANES))

            hn = h * _rms_scale(h)
            out_ref[rows, hs] = (hn * hg_ref[:, hs] * jax.nn.sigmoid(o_ref[rows, hs])).astype(BF16)


def _mlstm(pm, gcol, grow, conv_w, head_g, *, batch, seq, tb):
    t = pm.shape[0]
    assert seq % tb == 0 and tb % MLSTM_CHUNK == 0
    nb = seq // tb
    w2 = 2 * MLSTM_WIDTH
    blocks = (_nbytes((tb, w2), F32) + 2 * _nbytes((tb, MLSTM_WIDTH), F32) + _nbytes((tb, LANES), F32)
              + _nbytes((N_GATES, tb), F32) + _nbytes((tb, MLSTM_WIDTH), BF16))
    scratch = (_nbytes((tb + 2 * SUBLANES, w2), F32) + _nbytes((tb, w2), F32)
               + _nbytes((MLSTM_HEADS, HEAD_DIM, 2 * HEAD_DIM), F32))
    temps = 2 * _nbytes((tb, w2), F32)
    return pl.pallas_call(
        functools.partial(_mlstm_kernel, tb=tb),
        grid=(batch, nb),
        in_specs=[
            pl.BlockSpec((tb, w2), lambda b, i: (b * nb + i, 0)),
            pl.BlockSpec((tb, MLSTM_WIDTH), lambda b, i: (b * nb + i, 2)),
            pl.BlockSpec((tb, MLSTM_WIDTH), lambda b, i: (b * nb + i, 3)),
            pl.BlockSpec((tb, LANES), lambda b, i: (b * nb + i, 0)),
            pl.BlockSpec((1, N_GATES, tb), lambda b, i: (b, 0, i)),
            pl.BlockSpec((CONV_WIDTH, w2), lambda b, i: (0, 0)),
            pl.BlockSpec((1, MLSTM_WIDTH), lambda b, i: (0, 0)),
        ],
        out_specs=pl.BlockSpec((tb, MLSTM_WIDTH), lambda b, i: (b * nb + i, 0)),
        out_shape=jax.ShapeDtypeStruct((t, MLSTM_WIDTH), BF16),
        scratch_shapes=[
            pltpu.VMEM((tb + 2 * SUBLANES, w2), F32),
            pltpu.VMEM((tb, w2), F32),
            pltpu.VMEM((MLSTM_HEADS, HEAD_DIM, 2 * HEAD_DIM), F32),
            pltpu.VMEM((MLSTM_HEADS, SUBLANES, LANES), F32),
        ],
        compiler_params=pltpu.CompilerParams(
            dimension_semantics=("parallel", "arbitrary"),
            vmem_limit_bytes=_vmem_limit(blocks, scratch, temps)),
        name="mlstm",
    )(pm, pm, pm, gcol, grow, conv_w, head_g.reshape(1, MLSTM_WIDTH))


def _pool_kernel(u_ref, w_ref, scale_ref, out_ref, ext, *, ts):
    step = pl.program_id(1)
    halo = 2 * SUBLANES
    assert halo >= max(POOL_WINDOWS)

    @pl.when(step == 0)
    def _():
        ext[0:halo, :] = jnp.zeros((halo, POOL_WIDTH), F32)

    @pl.when(step > 0)
    def _():
        ext[0:halo, :] = ext[ts:ts + halo, :]

    ext[halo:halo + ts, :] = u_ref[...]
    pos = (step * ts + lax.broadcasted_iota(jnp.int32, (ts, 1), 0) + 1).astype(F32)
    for g, win in enumerate(POOL_WINDOWS):
        cols = slice(g * POOL_GROUP_DIM, (g + 1) * POOL_GROUP_DIM)
        acc = ext[halo:halo + ts, cols]
        for j in range(1, win):
            acc = acc + ext[halo - j:halo - j + ts, cols]
        mean = acc / jnp.minimum(pos, float(win))
        diff = (mean - ext[halo:halo + ts, cols]).astype(BF16)
        y = jnp.dot(diff, w_ref[g], preferred_element_type=F32)
        out_ref[:, cols] = (y * scale_ref[:, cols]).astype(BF16)


def _pool(pm, pool_w, pool_scale, *, batch, seq, ts):
    t = pm.shape[0]
    assert seq % ts == 0
    nb = seq // ts
    col_block = (4 * MLSTM_WIDTH) // POOL_WIDTH
    blocks = (_nbytes((ts, POOL_WIDTH), F32) + _nbytes(pool_w.shape, BF16)
              + _nbytes((ts, POOL_WIDTH), BF16))
    scratch = _nbytes((ts + 2 * SUBLANES, POOL_WIDTH), F32)
    temps = 4 * _nbytes((ts, POOL_GROUP_DIM), F32)
    return pl.pallas_call(
        functools.partial(_pool_kernel, ts=ts),
        grid=(batch, nb),
        in_specs=[
            pl.BlockSpec((ts, POOL_WIDTH), lambda b, i: (b * nb + i, col_block)),
            pl.BlockSpec(pool_w.shape, lambda b, i: (0, 0, 0)),
            pl.BlockSpec((1, POOL_WIDTH), lambda b, i: (0, 0)),
        ],
        out_specs=pl.BlockSpec((ts, POOL_WIDTH), lambda b, i: (b * nb + i, 0)),
        out_shape=jax.ShapeDtypeStruct((t, POOL_WIDTH), BF16),
        scratch_shapes=[pltpu.VMEM((ts + 2 * SUBLANES, POOL_WIDTH), F32)],
        compiler_params=pltpu.CompilerParams(
            dimension_semantics=("parallel", "arbitrary"),
            vmem_limit_bytes=_vmem_limit(blocks, scratch, temps)),
        name="pool",
    )(pm, pool_w, pool_scale.reshape(1, POOL_WIDTH))


def _split3_bf16(c):
    hi = c.astype(BF16).astype(F32)
    r = c - hi
    mid = r.astype(BF16).astype(F32)
    lo = (r - mid).astype(BF16).astype(F32)
    return hi, mid, lo


def _bias_rows(c_row, n, *, query_side):
    hi, mid, lo = _split3_bf16(c_row)
    ridx = lax.broadcasted_iota(jnp.int32, (HEAD_DIM, n), 0)
    sign = 1.0 if query_side else -1.0
    first = 0 if query_side else 3
    ones_at = 3 if query_side else 0
    rows = jnp.where(ridx == first, sign * hi, 0.0)
    rows = jnp.where(ridx == first + 1, sign * mid, rows)
    rows = jnp.where(ridx == first + 2, sign * lo, rows)
    return jnp.where((ridx >= ones_at) & (ridx < ones_at + 3), 1.0, rows)


def _fox_kernel(q_ref, k_ref, v_ref, c_ref, out_ref, kt_ref, va_ref, qa_ref, m_ref, acc_ref,
                *, tq, nq, n_parts):
    qi = pl.program_id(2)

    @pl.when(qi == 0)
    def _():
        ones_col = (lax.broadcasted_iota(jnp.int32, (tq, HEAD_DIM), 1) == 0).astype(BF16)
        for jb in range(nq):
            rows = slice(jb * tq, (jb + 1) * tq)
            kt = k_ref[rows, :].astype(F32).T
            ext = _bias_rows(c_ref[0, jb:jb + 1, :] * LOG2E, tq, query_side=False)
            kt_ref[jb] = jnp.concatenate([kt, ext], axis=0).astype(BF16)
            va_ref[rows, 0:HEAD_DIM] = v_ref[rows, :]
            va_ref[rows, HEAD_DIM:] = ones_col

    c2_q = c_ref[0, pl.ds(qi, 1), :] * LOG2E
    qa_ref[:, 0:HEAD_DIM] = q_ref[...]
    qa_ref[:, HEAD_DIM:] = _bias_rows(c2_q, tq, query_side=True).T.astype(BF16)
    m_ref[...] = jnp.full(m_ref.shape, -jnp.inf, F32)
    acc_ref[...] = jnp.zeros(acc_ref.shape, F32)

    def logits(r0, nr, kv_blk, nc, masked):
        s = jnp.dot(qa_ref[r0:r0 + nr, :], kt_ref[kv_blk][:, 0:nc], preferred_element_type=F32)
        if masked:
            keep = (lax.broadcasted_iota(jnp.int32, (nr, nc), 1)
                    <= lax.broadcasted_iota(jnp.int32, (nr, nc), 0) + r0)
            s = jnp.where(keep, s, -jnp.inf)
        return s

    def accumulate(s, r0, nr, kv_blk, nc):
        m_prev = m_ref[r0:r0 + nr, :]
        m_new = jnp.maximum(m_prev, jnp.max(s, axis=-1, keepdims=True))
        alpha = jnp.exp2(m_prev - m_new)
        p = jnp.exp2(s - m_new).astype(BF16)
        v_rows = pl.ds(pl.multiple_of(kv_blk * tq, tq), nc)
        acc_ref[r0:r0 + nr, :] = (alpha * acc_ref[r0:r0 + nr, :]
                                  + jnp.dot(p, va_ref[v_rows, :], preferred_element_type=F32))
        m_ref[r0:r0 + nr, :] = m_new

    def update(parts, kv_blk, masked):
        ss = [logits(r0, nr, kv_blk, nc, masked) for r0, nr, nc in parts]
        for s, (r0, nr, nc) in zip(ss, parts):
            accumulate(s, r0, nr, kv_blk, nc)

    pr = tq // n_parts

    def body(j, carry):
        update([(i * pr, pr, tq) for i in range(n_parts)], j, False)
        return carry

    lax.fori_loop(0, qi, body, 0)
    update([(i * pr, pr, (i + 1) * pr) for i in range(n_parts)], qi, True)
    out_ref[...] = (acc_ref[:, 0:HEAD_DIM] / acc_ref[:, HEAD_DIM:HEAD_DIM + 1]).astype(BF16)


def _fox(pa, grow, *, batch, seq, tq, n_parts):
    t = pa.shape[0]
    assert seq % tq == 0
    nq = seq // tq
    assert tq % n_parts == 0 and (tq // n_parts) % LANES == 0
    c = grow.reshape(batch * N_GATES, nq, tq)
    gate0 = 2 * MLSTM_HEADS
    blocks = (2 * _nbytes((tq, HEAD_DIM), BF16) + 2 * _nbytes((seq, HEAD_DIM), BF16)
              + _nbytes((max(nq, SUBLANES), tq), F32))
    scratch = (2 * _nbytes((seq, 2 * HEAD_DIM), BF16) + _nbytes((tq, 2 * HEAD_DIM), BF16)
               + _nbytes((tq, LANES), F32) + _nbytes((tq, 2 * HEAD_DIM), F32))
    temps = 3 * _nbytes((tq, tq), F32)
    return pl.pallas_call(
        functools.partial(_fox_kernel, tq=tq, nq=nq, n_parts=n_parts),
        grid=(batch, FOX_HEADS, nq),
        in_specs=[
            pl.BlockSpec((tq, HEAD_DIM), lambda b, h, i: (b * nq + i, h)),
            pl.BlockSpec((seq, HEAD_DIM), lambda b, h, i: (b, FOX_HEADS + h)),
            pl.BlockSpec((seq, HEAD_DIM), lambda b, h, i: (b, 2 * FOX_HEADS + h)),
            pl.BlockSpec((1, nq, tq), lambda b, h, i: (b * N_GATES + gate0 + h, 0, 0)),
        ],
        out_specs=pl.BlockSpec((tq, HEAD_DIM), lambda b, h, i: (b * nq + i, h)),
        out_shape=jax.ShapeDtypeStruct((t, FOX_WIDTH), BF16),
        scratch_shapes=[
            pltpu.VMEM((nq, 2 * HEAD_DIM, tq), BF16),
            pltpu.VMEM((seq, 2 * HEAD_DIM), BF16),
            pltpu.VMEM((tq, 2 * HEAD_DIM), BF16),
            pltpu.VMEM((tq, 1), F32),
            pltpu.VMEM((tq, 2 * HEAD_DIM), F32),
        ],
        compiler_params=pltpu.CompilerParams(
            dimension_semantics=("parallel", "parallel", "arbitrary"),
            vmem_limit_bytes=_vmem_limit(blocks, scratch, temps)),
        name="fox",
    )(pa, pa, pa, c)


def _proj_out_kernel(hm_ref, hp_ref, ha_ref, w_ref, x_ref, g_ref, o_ref):
    tm = x_ref.shape[0]
    y = jnp.dot(hm_ref[...], w_ref[0:MLSTM_WIDTH, :], preferred_element_type=F32)
    y = y + jnp.dot(hp_ref[...], w_ref[MLSTM_WIDTH:MLSTM_WIDTH + POOL_WIDTH, :],
                    preferred_element_type=F32)
    y = y + jnp.dot(ha_ref[...], w_ref[MLSTM_WIDTH + POOL_WIDTH:, :], preferred_element_type=F32)
    o_ref[...] = y
    _postnorm_residual(x_ref, o_ref, g_ref, o_ref, 1.0)


def _proj_out(hm, hp, ha, w_out, x, g, *, tm):
    t, d = x.shape
    assert t % tm == 0
    blocks = (_nbytes((tm, d), BF16) + _nbytes((d, d), BF16) + 2 * _nbytes((tm, d), F32))
    temps = 2 * _nbytes((tm, d), F32)
    return pl.pallas_call(
        _proj_out_kernel,
        grid=(t // tm,),
        in_specs=[
            pl.BlockSpec((tm, MLSTM_WIDTH), lambda i: (i, 0)),
            pl.BlockSpec((tm, POOL_WIDTH), lambda i: (i, 0)),
            pl.BlockSpec((tm, FOX_WIDTH), lambda i: (i, 0)),
            pl.BlockSpec((d, d), lambda i: (0, 0)),
            pl.BlockSpec((tm, d), lambda i: (i, 0)),
            pl.BlockSpec((1, d), lambda i: (0, 0)),
        ],
        out_specs=pl.BlockSpec((tm, d), lambda i: (i, 0)),
        out_shape=jax.ShapeDtypeStruct((t, d), F32),
        compiler_params=pltpu.CompilerParams(
            dimension_semantics=("parallel",),
            vmem_limit_bytes=_vmem_limit(blocks, 0, temps)),
        name="proj_out",
    )(hm, hp, ha, w_out, x, g.reshape(1, d))


def _mixer(x, pre_g, post_g, w_in_parts, conv_w, b_i, b_f, head_g, pool_w, pool_scale, fox_b_f,
           w_out, *, batch, seq):
    w_pm, w_pa, w_gate = w_in_parts
    bias = jnp.concatenate([b_i, b_f, fox_b_f]).reshape(N_GATES, 1)

    pm, pa, pg = _proj_in(x, pre_g, w_pm, w_pa, w_gate, tm=256)
    grow, gcol = _gates(pg, bias, batch=batch, seq=seq)
    hm = _mlstm(pm, gcol, grow, conv_w, head_g, batch=batch, seq=seq, tb=min(512, seq))
    hp = _pool(pm, pool_w.astype(BF16), pool_scale, batch=batch, seq=seq, ts=min(512, seq))
    ha = _fox(pa, grow, batch=batch, seq=seq, tq=min(1024, seq), n_parts=4)
    return _proj_out(hm, hp, ha, w_out, x, post_g, tm=512)


def kernel(x, ffn1_pre_g, ffn1_post_g, ffn1_w_gate, ffn1_w_up, ffn1_w_down, mix_pre_g, mix_post_g,
           w_in, mlstm_conv, mlstm_b_i, mlstm_b_f, mlstm_head_g, pool_w, pool_scale, fox_b_f, w_out,
           ffn2_pre_g, ffn2_post_g, ffn2_w_gate, ffn2_w_up, ffn2_w_down):
    batch, seq, d = x.shape
    depth = w_in.shape[0]
    xt = x.reshape(batch * seq, d)
    for l in range(depth):
        xt = _ffn(xt, ffn1_pre_g[l], ffn1_post_g[l], _layer_bf16(ffn1_w_gate, l),
                  _layer_bf16(ffn1_w_up, l), _layer_bf16(ffn1_w_down, l), tm=1024, tf=512)
        xt = _mixer(xt, mix_pre_g[l], mix_post_g[l], _split_w_in(w_in, l, tr=256), mlstm_conv[l],
                    mlstm_b_i[l],
                    mlstm_b_f[l], mlstm_head_g[l], pool_w[l], pool_scale[l], fox_b_f[l],
                    _layer_bf16(w_out, l), batch=batch, seq=seq)
        xt = _ffn(xt, ffn2_pre_g[l], ffn2_post_g[l], _layer_bf16(ffn2_w_gate, l),
                  _layer_bf16(ffn2_w_up, l), _layer_bf16(ffn2_w_down, l), tm=1024, tf=512)
    return xt.reshape(batch, seq, d)
```

```python
import functools

import jax
import jax.numpy as jnp
from jax import lax
from jax.experimental import pallas as pl
from jax.experimental.pallas import tpu as pltpu

F32 = jnp.float32
BF16 = jnp.bfloat16

D_MODEL = 2048
HEAD_DIM = 128
MLSTM_WIDTH = D_MODEL // 4
MLSTM_HEADS = MLSTM_WIDTH // HEAD_DIM
POOL_WIDTH = D_MODEL // 4
POOL_WINDOWS = (2, 4, 8, 16)
POOL_GROUP_DIM = POOL_WIDTH // len(POOL_WINDOWS)
FOX_WIDTH = D_MODEL - MLSTM_WIDTH - POOL_WIDTH
FOX_HEADS = FOX_WIDTH // HEAD_DIM
CONV_WIDTH = 4
MLSTM_CHUNK = 128
RMS_EPS = 1e-6
FFN_RESIDUAL_WEIGHT = 0.5

OFF_MI = 4 * MLSTM_WIDTH
OFF_MF = OFF_MI + MLSTM_HEADS
OFF_POOL = OFF_MF + MLSTM_HEADS
OFF_AQ = OFF_POOL + POOL_WIDTH
OFF_AF = OFF_AQ + 3 * FOX_WIDTH
N_IN = OFF_AF + FOX_HEADS

PM_WIDTH = 4 * MLSTM_WIDTH + POOL_WIDTH
PA_WIDTH = 3 * FOX_WIDTH
N_GATES = 2 * MLSTM_HEADS + FOX_HEADS
LOG2E = 1.4426950408889634
FOX_Q_SCALE = LOG2E * HEAD_DIM ** -0.5
LANES = 128
SUBLANES = 8
VMEM_LIMIT_CAP = 58 * 1024 * 1024


def _vmem_limit(block_bytes, scratch_bytes, temp_bytes):
    return int(min(2 * block_bytes + scratch_bytes + temp_bytes + (4 << 20), VMEM_LIMIT_CAP))


def _nbytes(shape, dtype):
    n = 1
    for s in shape:
        n *= s
    return n * jnp.dtype(dtype).itemsize


def _rms_scale(x):
    return lax.rsqrt(jnp.mean(x * x, axis=-1, keepdims=True) + RMS_EPS)


CAST_BLOCK_BYTES = 6 << 20


def _cast_kernel(w_ref, o_ref):
    o_ref[...] = w_ref[...].astype(BF16)


def _layer_bf16(w, layer):
    _, r, c = w.shape
    tr = r
    while _nbytes((tr, c), F32) > CAST_BLOCK_BYTES and tr % 2 == 0 and (tr // 2) % 16 == 0:
        tr //= 2
    assert r % tr == 0
    blocks = _nbytes((tr, c), F32) + _nbytes((tr, c), BF16)
    return pl.pallas_call(
        _cast_kernel,
        grid=(r // tr,),
        in_specs=[pl.BlockSpec((None, tr, c), lambda i: (layer, i, 0))],
        out_specs=pl.BlockSpec((tr, c), lambda i: (i, 0)),
        out_shape=jax.ShapeDtypeStruct((r, c), BF16),
        compiler_params=pltpu.CompilerParams(
            dimension_semantics=("parallel",),
            vmem_limit_bytes=_vmem_limit(blocks, 0, 0)),
        name="cast",
    )(w)


def _split_w_in_kernel(w_ref, pm_ref, pa_ref, gate_ref):
    pm_ref[:, 0:OFF_MI] = w_ref[:, 0:OFF_MI].astype(BF16)
    pm_ref[:, OFF_MI:PM_WIDTH] = w_ref[:, OFF_POOL:OFF_AQ].astype(BF16)
    pa_ref[...] = w_ref[:, OFF_AQ:OFF_AF].astype(BF16)
    gates = jnp.concatenate([w_ref[:, OFF_MI:OFF_POOL], w_ref[:, OFF_AF:N_IN],
                             jnp.zeros((w_ref.shape[0], LANES - N_GATES), F32)], axis=1)
    gate_ref[...] = gates.astype(BF16)


def _split_w_in(w_in, layer, *, tr):
    _, r, c = w_in.shape
    assert r % tr == 0 and c == N_IN
    blocks = _nbytes((tr, c), F32) + _nbytes((tr, PM_WIDTH + PA_WIDTH + LANES), BF16)
    return pl.pallas_call(
        _split_w_in_kernel,
        grid=(r // tr,),
        in_specs=[pl.BlockSpec((None, tr, c), lambda i: (layer, i, 0))],
        out_specs=[
            pl.BlockSpec((tr, PM_WIDTH), lambda i: (i, 0)),
            pl.BlockSpec((tr, PA_WIDTH), lambda i: (i, 0)),
            pl.BlockSpec((tr, LANES), lambda i: (i, 0)),
        ],
        out_shape=[
            jax.ShapeDtypeStruct((r, PM_WIDTH), BF16),
            jax.ShapeDtypeStruct((r, PA_WIDTH), BF16),
            jax.ShapeDtypeStruct((r, LANES), BF16),
        ],
        compiler_params=pltpu.CompilerParams(
            dimension_semantics=("parallel",),
            vmem_limit_bytes=_vmem_limit(blocks, 0, 2 * _nbytes((tr, c), F32))),
        name="split_w_in",
    )(w_in)


NORM_ROWS = 32
NORM_COLS = 512


def _row_rms_scale(ref, rows):
    d = ref.shape[1]
    ss = None
    for c0 in range(0, d, NORM_COLS):
        v = ref[rows, c0:c0 + NORM_COLS]
        ss = v * v if ss is None else ss + v * v
    return lax.rsqrt(jnp.sum(ss, axis=-1, keepdims=True) * (1.0 / d) + RMS_EPS)


def _normalise_rows(n_rows, src_ref, apply_fn):
    groups = 2
    trip_rows = groups * NORM_ROWS
    n_trips = n_rows // trip_rows

    def rows_of(r):
        base = pl.multiple_of(r * trip_rows, trip_rows)
        return [pl.ds(base + k * NORM_ROWS, NORM_ROWS) for k in range(groups)]

    def scales_of(r):
        return tuple(_row_rms_scale(src_ref, rows) for rows in rows_of(r))

    def body(r, scales):
        nxt = scales_of(jnp.minimum(r + 1, n_trips - 1))
        for rows, scale in zip(rows_of(r), scales):
            apply_fn(rows, scale)
        return nxt

    lax.fori_loop(0, n_trips, body, scales_of(0))


def _prenorm(x_ref, g_ref, h_ref):
    def apply(rows, scale):
        for c0 in range(0, x_ref.shape[1], NORM_COLS):
            cols = slice(c0, c0 + NORM_COLS)
            h_ref[rows, cols] = (x_ref[rows, cols] * scale * g_ref[:, cols]).astype(BF16)
    _normalise_rows(x_ref.shape[0], x_ref, apply)


def _postnorm_residual(x_ref, y_ref, g_ref, o_ref, weight):
    def apply(rows, scale):
        scale = scale * weight
        for c0 in range(0, x_ref.shape[1], NORM_COLS):
            cols = slice(c0, c0 + NORM_COLS)
            o_ref[rows, cols] = x_ref[rows, cols] + y_ref[rows, cols] * scale * g_ref[:, cols]
    _normalise_rows(x_ref.shape[0], y_ref, apply)


def _ffn_kernel(x_ref, pre_g_ref, post_g_ref, wg_ref, wu_ref, wd_ref, o_ref, h_ref, *, n_f):
    j = pl.program_id(1)
    tm = x_ref.shape[0]

    @pl.when(j == 0)
    def _():
        _prenorm(x_ref, pre_g_ref, h_ref)
        o_ref[...] = jnp.zeros(o_ref.shape, F32)

    h = h_ref[...]
    g = jnp.dot(h, wg_ref[...], preferred_element_type=F32)
    u = jnp.dot(h, wu_ref[...], preferred_element_type=F32)
    a = (g * jax.nn.sigmoid(g) * u).astype(BF16)
    o_ref[...] += jnp.dot(a, wd_ref[...], preferred_element_type=F32)

    @pl.when(j == n_f - 1)
    def _():
        _postnorm_residual(x_ref, o_ref, post_g_ref, o_ref, FFN_RESIDUAL_WEIGHT)


def _ffn(x, pre_g, post_g, wg, wu, wd, *, tm, tf):
    t, d = x.shape
    f = wg.shape[1]
    assert t % tm == 0 and f % tf == 0 and FFN_RESIDUAL_WEIGHT == 0.5
    blocks = (_nbytes((tm, d), F32) * 2 + 2 * _nbytes((d, tf), BF16) + _nbytes((tf, d), BF16))
    scratch = _nbytes((tm, d), BF16)
    temps = 3 * _nbytes((tm, tf), F32)
    return pl.pallas_call(
        functools.partial(_ffn_kernel, n_f=f // tf),
        grid=(t // tm, f // tf),
        in_specs=[
            pl.BlockSpec((tm, d), lambda i, j: (i, 0)),
            pl.BlockSpec((1, d), lambda i, j: (0, 0)),
            pl.BlockSpec((1, d), lambda i, j: (0, 0)),
            pl.BlockSpec((d, tf), lambda i, j: (0, j)),
            pl.BlockSpec((d, tf), lambda i, j: (0, j)),
            pl.BlockSpec((tf, d), lambda i, j: (j, 0)),
        ],
        out_specs=pl.BlockSpec((tm, d), lambda i, j: (i, 0)),
        out_shape=jax.ShapeDtypeStruct((t, d), F32),
        scratch_shapes=[pltpu.VMEM((tm, d), BF16)],
        compiler_params=pltpu.CompilerParams(
            dimension_semantics=("parallel", "arbitrary"),
            vmem_limit_bytes=_vmem_limit(blocks, scratch, temps)),
        name="ffn",
    )(x, pre_g.reshape(1, d), post_g.reshape(1, d), wg, wu, wd)


def _proj_in_kernel(x_ref, g_ref, wpm_ref, wpa_ref, wgate_ref, pm_ref, pa_ref, pg_ref):
    x = x_ref[...]
    h = (x * _rms_scale(x) * g_ref[...]).astype(BF16)
    pm_ref[...] = jnp.dot(h, wpm_ref[...], preferred_element_type=F32)
    pa = jnp.dot(h, wpa_ref[...], preferred_element_type=F32)
    pa_ref[:, 0:FOX_WIDTH] = (pa[:, 0:FOX_WIDTH] * FOX_Q_SCALE).astype(BF16)
    pa_ref[:, FOX_WIDTH:] = pa[:, FOX_WIDTH:].astype(BF16)
    pg_ref[...] = jnp.dot(h, wgate_ref[...], preferred_element_type=F32)


def _proj_in(x, g, w_pm, w_pa, w_gate, *, tm):
    t, d = x.shape
    assert t % tm == 0 and w_pm.shape == (d, PM_WIDTH) and w_pa.shape == (d, PA_WIDTH)
    resident = pl.Buffered(1)
    blocks = (_nbytes((tm, d), F32) + _nbytes((d, LANES), BF16) + _nbytes((tm, PM_WIDTH), F32)
              + _nbytes((tm, PA_WIDTH), BF16) + _nbytes((tm, LANES), F32))
    scratch = _nbytes((d, PM_WIDTH + PA_WIDTH), BF16)
    temps = _nbytes((tm, d), BF16) + 2 * _nbytes((tm, PA_WIDTH), F32)
    return pl.pallas_call(
        _proj_in_kernel,
        grid=(t // tm,),
        in_specs=[
            pl.BlockSpec((tm, d), lambda i: (i, 0)),
            pl.BlockSpec((1, d), lambda i: (0, 0)),
            pl.BlockSpec((d, PM_WIDTH), lambda i: (0, 0), pipeline_mode=resident),
            pl.BlockSpec((d, PA_WIDTH), lambda i: (0, 0), pipeline_mode=resident),
            pl.BlockSpec((d, LANES), lambda i: (0, 0)),
        ],
        out_specs=[
            pl.BlockSpec((tm, PM_WIDTH), lambda i: (i, 0)),
            pl.BlockSpec((tm, PA_WIDTH), lambda i: (i, 0)),
            pl.BlockSpec((tm, LANES), lambda i: (i, 0)),
        ],
        out_shape=[
            jax.ShapeDtypeStruct((t, PM_WIDTH), F32),
            jax.ShapeDtypeStruct((t, PA_WIDTH), BF16),
            jax.ShapeDtypeStruct((t, LANES), F32),
        ],
        compiler_params=pltpu.CompilerParams(
            dimension_semantics=("parallel",),
            vmem_limit_bytes=_vmem_limit(blocks, scratch, temps)),
        name="proj_in",
    )(x, g.reshape(1, d), w_pm, w_pa, w_gate)


def _log_sigmoid(z):
    return -(jnp.maximum(-z, 0.0) + jnp.log1p(jnp.exp(-jnp.abs(z))))


def _lane_scan(v, shifts, lane_pos):
    for sh in shifts:
        v = v + jnp.where(lane_pos >= sh, pltpu.roll(v, sh, axis=1), 0.0)
    return v


def _gates_kernel(pg_ref, bias_ref, row_ref, col_ref):
    s = pg_ref.shape[0]
    z = pg_ref[...].T[0:N_GATES, :] + bias_ref[...]
    lane = lax.broadcasted_iota(jnp.int32, (1, s), 1)
    mi = z[0:MLSTM_HEADS]
    lf_m = _log_sigmoid(z[MLSTM_HEADS:2 * MLSTM_HEADS])
    lf_a = _log_sigmoid(z[2 * MLSTM_HEADS:N_GATES])
    chunk_shifts = [1 << k for k in range(MLSTM_CHUNK.bit_length() - 1)]
    seq_shifts = [1 << k for k in range((s - 1).bit_length())]
    bcum = _lane_scan(lf_m, chunk_shifts, lane % MLSTM_CHUNK)
    cfox = _lane_scan(lf_a, seq_shifts, lane)
    rows = jnp.concatenate([mi, bcum, cfox], axis=0)
    row_ref[0] = rows
    padded = jnp.concatenate([rows, jnp.zeros((LANES - N_GATES, s), F32)], axis=0)
    col_ref[...] = padded.T


def _gates(pg, bias, *, batch, seq):
    t = pg.shape[0]
    assert t == batch * seq
    blocks = 2 * _nbytes((seq, LANES), F32) + _nbytes((N_GATES, seq), F32)
    temps = 6 * _nbytes((seq, LANES), F32)
    return pl.pallas_call(
        _gates_kernel,
        grid=(batch,),
        in_specs=[
            pl.BlockSpec((seq, LANES), lambda b: (b, 0)),
            pl.BlockSpec((N_GATES, 1), lambda b: (0, 0)),
        ],
        out_specs=[
            pl.BlockSpec((1, N_GATES, seq), lambda b: (b, 0, 0)),
            pl.BlockSpec((seq, LANES), lambda b: (b, 0)),
        ],
        out_shape=[
            jax.ShapeDtypeStruct((batch, N_GATES, seq), F32),
            jax.ShapeDtypeStruct((t, LANES), F32),
        ],
        compiler_params=pltpu.CompilerParams(
            dimension_semantics=("parallel",),
            vmem_limit_bytes=_vmem_limit(blocks, 0, temps)),
        name="gates",
    )(pg, bias)


def _mlstm_kernel(qk_ref, v_ref, o_ref, gcol_ref, grow_ref, convw_ref, hg_ref, out_ref,
                  ubuf, qkc, cn_ref, m_ref, *, tb):
    step = pl.program_id(1)
    L = MLSTM_CHUNK
    halo = SUBLANES

    @pl.when(step == 0)
    def _():
        ubuf[0:halo, :] = jnp.zeros((halo, 2 * MLSTM_WIDTH), F32)
        cn_ref[...] = jnp.zeros(cn_ref.shape, F32)
        m_ref[...] = jnp.zeros(m_ref.shape, F32)

    @pl.when(step > 0)
    def _():
        ubuf[0:halo, :] = ubuf[tb:tb + halo, :]

    ubuf[halo:halo + tb, :] = qk_ref[...]
    w = convw_ref[...]
    y = ubuf[halo:halo + tb, :] * w[0:1, :]
    for j in range(1, CONV_WIDTH):
        y = y + ubuf[halo - j:halo - j + tb, :] * w[j:j + 1, :]
    qkc[...] = y * jax.nn.sigmoid(y)

    tri = (lax.broadcasted_iota(jnp.int32, (L, L), 0) >= lax.broadcasted_iota(jnp.int32, (L, L), 1))
    ones_col = (lax.broadcasted_iota(jnp.int32, (L, HEAD_DIM), 1) == 0).astype(F32)
    q_scale = HEAD_DIM ** -0.5

    for hd in range(MLSTM_HEADS):
        hs = slice(hd * HEAD_DIM, (hd + 1) * HEAD_DIM)
        ks = slice(MLSTM_WIDTH + hd * HEAD_DIM, MLSTM_WIDTH + (hd + 1) * HEAD_DIM)
        for c in range(tb // L):
            rows = slice(c * L, (c + 1) * L)
            q = (qkc[rows, hs] * q_scale).astype(BF16)
            k = qkc[rows, ks]
            k_bf = k.astype(BF16)
            vaug = jnp.concatenate([v_ref[rows, hs], ones_col], axis=1)
            i_col = gcol_ref[rows, hd:hd + 1]
            b_col = gcol_ref[rows, MLSTM_HEADS + hd:MLSTM_HEADS + hd + 1]
            i_row = grow_ref[0, hd:hd + 1, rows]
            b_row = grow_ref[0, MLSTM_HEADS + hd:MLSTM_HEADS + hd + 1, rows]
            m_prev = m_ref[hd][0:1, 0:1]
            cn = cn_ref[hd]

            log_d = jnp.where(tri, b_col - b_row + i_row, -jnp.inf)
            inter = b_col + m_prev
            m_t = jnp.maximum(inter, jnp.max(log_d, axis=-1, keepdims=True))
            s = lax.dot_general(q, k_bf, (((1,), (1,)), ((), ())), preferred_element_type=F32)
            scores = (s * jnp.exp(log_d - m_t)).astype(BF16)
            inter_w = jnp.exp(inter - m_t)
            numden = (jnp.dot(scores, vaug.astype(BF16), preferred_element_type=F32)
                      + inter_w * jnp.dot(q, cn.astype(BF16), preferred_element_type=F32))
            num = numden[:, 0:HEAD_DIM]
            den = numden[:, HEAD_DIM:HEAD_DIM + 1]
            h = num / jnp.maximum(jnp.abs(den), jnp.exp(-m_t))

            b_last = b_row[:, L - 1:L]
            log_w = b_last - b_col + i_col
            m_new = jnp.maximum(b_last + m_prev, jnp.max(log_w, axis=0, keepdims=True))
            wv = (jnp.exp(log_w - m_new) * vaug).astype(BF16)
            decay = jnp.exp(b_last + m_prev - m_new)
            cn_ref[hd] = decay * cn + jnp.dot(k.T.astype(BF16), wv, preferred_element_type=F32)
            m_ref[hd] = jnp.broadcast_to(m_new, (SUBLANES, LANES))

            hn = h * _rms_scale(h)
            out_ref[rows, hs] = (hn * hg_ref[:, hs] * jax.nn.sigmoid(o_ref[rows, hs])).astype(BF16)


def _mlstm(pm, gcol, grow, conv_w, head_g, *, batch, seq, tb):
    t = pm.shape[0]
    assert seq % tb == 0 and tb % MLSTM_CHUNK == 0
    nb = seq // tb
    w2 = 2 * MLSTM_WIDTH
    blocks = (_nbytes((tb, w2), F32) + 2 * _nbytes((tb, MLSTM_WIDTH), F32) + _nbytes((tb, LANES), F32)
              + _nbytes((N_GATES, tb), F32) + _nbytes((tb, MLSTM_WIDTH), BF16))
    scratch = (_nbytes((tb + 2 * SUBLANES, w2), F32) + _nbytes((tb, w2), F32)
               + _nbytes((MLSTM_HEADS, HEAD_DIM, 2 * HEAD_DIM), F32))
    temps = 2 * _nbytes((tb, w2), F32)
    return pl.pallas_call(
        functools.partial(_mlstm_kernel, tb=tb),
        grid=(batch, nb),
        in_specs=[
            pl.BlockSpec((tb, w2), lambda b, i: (b * nb + i, 0)),
            pl.BlockSpec((tb, MLSTM_WIDTH), lambda b, i: (b * nb + i, 2)),
            pl.BlockSpec((tb, MLSTM_WIDTH), lambda b, i: (b * nb + i, 3)),
            pl.BlockSpec((tb, LANES), lambda b, i: (b * nb + i, 0)),
            pl.BlockSpec((1, N_GATES, tb), lambda b, i: (b, 0, i)),
            pl.BlockSpec((CONV_WIDTH, w2), lambda b, i: (0, 0)),
            pl.BlockSpec((1, MLSTM_WIDTH), lambda b, i: (0, 0)),
        ],
        out_specs=pl.BlockSpec((tb, MLSTM_WIDTH), lambda b, i: (b * nb + i, 0)),
        out_shape=jax.ShapeDtypeStruct((t, MLSTM_WIDTH), BF16),
        scratch_shapes=[
            pltpu.VMEM((tb + 2 * SUBLANES, w2), F32),
            pltpu.VMEM((tb, w2), F32),
            pltpu.VMEM((MLSTM_HEADS, HEAD_DIM, 2 * HEAD_DIM), F32),
            pltpu.VMEM((MLSTM_HEADS, SUBLANES, LANES), F32),
        ],
        compiler_params=pltpu.CompilerParams(
            dimension_semantics=("parallel", "arbitrary"),
            vmem_limit_bytes=_vmem_limit(blocks, scratch, temps)),
        name="mlstm",
    )(pm, pm, pm, gcol, grow, conv_w, head_g.reshape(1, MLSTM_WIDTH))


def _pool_kernel(u_ref, w_ref, scale_ref, out_ref, ext, *, ts):
    step = pl.program_id(1)
    halo = 2 * SUBLANES
    assert halo >= max(POOL_WINDOWS)

    @pl.when(step == 0)
    def _():
        ext[0:halo, :] = jnp.zeros((halo, POOL_WIDTH), F32)

    @pl.when(step > 0)
    def _():
        ext[0:halo, :] = ext[ts:ts + halo, :]

    ext[halo:halo + ts, :] = u_ref[...]
    pos = (step * ts + lax.broadcasted_iota(jnp.int32, (ts, 1), 0) + 1).astype(F32)
    for g, win in enumerate(POOL_WINDOWS):
        cols = slice(g * POOL_GROUP_DIM, (g + 1) * POOL_GROUP_DIM)
        acc = ext[halo:halo + ts, cols]
        for j in range(1, win):
            acc = acc + ext[halo - j:halo - j + ts, cols]
        mean = acc / jnp.minimum(pos, float(win))
        diff = (mean - ext[halo:halo + ts, cols]).astype(BF16)
        y = jnp.dot(diff, w_ref[g], preferred_element_type=F32)
        out_ref[:, cols] = (y * scale_ref[:, cols]).astype(BF16)


def _pool(pm, pool_w, pool_scale, *, batch, seq, ts):
    t = pm.shape[0]
    assert seq % ts == 0
    nb = seq // ts
    col_block = (4 * MLSTM_WIDTH) // POOL_WIDTH
    blocks = (_nbytes((ts, POOL_WIDTH), F32) + _nbytes(pool_w.shape, BF16)
              + _nbytes((ts, POOL_WIDTH), BF16))
    scratch = _nbytes((ts + 2 * SUBLANES, POOL_WIDTH), F32)
    temps = 4 * _nbytes((ts, POOL_GROUP_DIM), F32)
    return pl.pallas_call(
        functools.partial(_pool_kernel, ts=ts),
        grid=(batch, nb),
        in_specs=[
            pl.BlockSpec((ts, POOL_WIDTH), lambda b, i: (b * nb + i, col_block)),
            pl.BlockSpec(pool_w.shape, lambda b, i: (0, 0, 0)),
            pl.BlockSpec((1, POOL_WIDTH), lambda b, i: (0, 0)),
        ],
        out_specs=pl.BlockSpec((ts, POOL_WIDTH), lambda b, i: (b * nb + i, 0)),
        out_shape=jax.ShapeDtypeStruct((t, POOL_WIDTH), BF16),
        scratch_shapes=[pltpu.VMEM((ts + 2 * SUBLANES, POOL_WIDTH), F32)],
        compiler_params=pltpu.CompilerParams(
            dimension_semantics=("parallel", "arbitrary"),
            vmem_limit_bytes=_vmem_limit(blocks, scratch, temps)),
        name="pool",
    )(pm, pool_w, pool_scale.reshape(1, POOL_WIDTH))


def _split3_bf16(c):
    hi = c.astype(BF16).astype(F32)
    r = c - hi
    mid = r.astype(BF16).astype(F32)
    lo = (r - mid).astype(BF16).astype(F32)
    return hi, mid, lo


def _bias_rows(c_row, n, *, query_side):
    hi, mid, lo = _split3_bf16(c_row)
    ridx = lax.broadcasted_iota(jnp.int32, (HEAD_DIM, n), 0)
    sign = 1.0 if query_side else -1.0
    first = 0 if query_side else 3
    ones_at = 3 if query_side else 0
    rows = jnp.where(ridx == first, sign * hi, 0.0)
    rows = jnp.where(ridx == first + 1, sign * mid, rows)
    rows = jnp.where(ridx == first + 2, sign * lo, rows)
    return jnp.where((ridx >= ones_at) & (ridx < ones_at + 3), 1.0, rows)


def _fox_kernel(q_ref, k_ref, v_ref, c_ref, out_ref, kt_ref, va_ref, qa_ref, m_ref, acc_ref,
                *, tq, nq, n_parts):
    qi = pl.program_id(2)

    @pl.when(qi == 0)
    def _():
        ones_col = (lax.broadcasted_iota(jnp.int32, (tq, HEAD_DIM), 1) == 0).astype(BF16)
        for jb in range(nq):
            rows = slice(jb * tq, (jb + 1) * tq)
            kt = k_ref[rows, :].astype(F32).T
            ext = _bias_rows(c_ref[0, jb:jb + 1, :] * LOG2E, tq, query_side=False)
            kt_ref[jb] = jnp.concatenate([kt, ext], axis=0).astype(BF16)
            va_ref[rows, 0:HEAD_DIM] = v_ref[rows, :]
            va_ref[rows, HEAD_DIM:] = ones_col

    c2_q = c_ref[0, pl.ds(qi, 1), :] * LOG2E
    qa_ref[:, 0:HEAD_DIM] = q_ref[...]
    qa_ref[:, HEAD_DIM:] = _bias_rows(c2_q, tq, query_side=True).T.astype(BF16)
    m_ref[...] = jnp.full(m_ref.shape, -jnp.inf, F32)
    acc_ref[...] = jnp.zeros(acc_ref.shape, F32)

    def logits(r0, nr, kv_blk, nc, masked):
        s = jnp.dot(qa_ref[r0:r0 + nr, :], kt_ref[kv_blk][:, 0:nc], preferred_element_type=F32)
        if masked:
            keep = (lax.broadcasted_iota(jnp.int32, (nr, nc), 1)
                    <= lax.broadcasted_iota(jnp.int32, (nr, nc), 0) + r0)
            s = jnp.where(keep, s, -jnp.inf)
        return s

    def accumulate(s, r0, nr, kv_blk, nc):
        m_prev = m_ref[r0:r0 + nr, :]
        m_new = jnp.maximum(m_prev, jnp.max(s, axis=-1, keepdims=True))
        alpha = jnp.exp2(m_prev - m_new)
        p = jnp.exp2(s - m_new).astype(BF16)
        v_rows = pl.ds(pl.multiple_of(kv_blk * tq, tq), nc)
        acc_ref[r0:r0 + nr, :] = (alpha * acc_ref[r0:r0 + nr, :]
                                  + jnp.dot(p, va_ref[v_rows, :], preferred_element_type=F32))
        m_ref[r0:r0 + nr, :] = m_new

    def update(parts, kv_blk, masked):
        ss = [logits(r0, nr, kv_blk, nc, masked) for r0, nr, nc in parts]
        for s, (r0, nr, nc) in zip(ss, parts):
            accumulate(s, r0, nr, kv_blk, nc)

    pr = tq // n_parts

    def body(j, carry):
        update([(i * pr, pr, tq) for i in range(n_parts)], j, False)
        return carry

    lax.fori_loop(0, qi, body, 0)
    update([(i * pr, pr, (i + 1) * pr) for i in range(n_parts)], qi, True)
    out_ref[...] = (acc_ref[:, 0:HEAD_DIM] / acc_ref[:, HEAD_DIM:HEAD_DIM + 1]).astype(BF16)


def _fox(pa, grow, *, batch, seq, tq, n_parts):
    t = pa.shape[0]
    assert seq % tq == 0
    nq = seq // tq
    assert tq % n_parts == 0 and (tq // n_parts) % LANES == 0
    c = grow.reshape(batch * N_GATES, nq, tq)
    gate0 = 2 * MLSTM_HEADS
    blocks = (2 * _nbytes((tq, HEAD_DIM), BF16) + 2 * _nbytes((seq, HEAD_DIM), BF16)
              + _nbytes((max(nq, SUBLANES), tq), F32))
    scratch = (2 * _nbytes((seq, 2 * HEAD_DIM), BF16) + _nbytes((tq, 2 * HEAD_DIM), BF16)
               + _nbytes((tq, LANES), F32) + _nbytes((tq, 2 * HEAD_DIM), F32))
    temps = 3 * _nbytes((tq, tq), F32)
    return pl.pallas_call(
        functools.partial(_fox_kernel, tq=tq, nq=nq, n_parts=n_parts),
        grid=(batch, FOX_HEADS, nq),
        in_specs=[
            pl.BlockSpec((tq, HEAD_DIM), lambda b, h, i: (b * nq + i, h)),
            pl.BlockSpec((seq, HEAD_DIM), lambda b, h, i: (b, FOX_HEADS + h)),
            pl.BlockSpec((seq, HEAD_DIM), lambda b, h, i: (b, 2 * FOX_HEADS + h)),
            pl.BlockSpec((1, nq, tq), lambda b, h, i: (b * N_GATES + gate0 + h, 0, 0)),
        ],
        out_specs=pl.BlockSpec((tq, HEAD_DIM), lambda b, h, i: (b * nq + i, h)),
        out_shape=jax.ShapeDtypeStruct((t, FOX_WIDTH), BF16),
        scratch_shapes=[
            pltpu.VMEM((nq, 2 * HEAD_DIM, tq), BF16),
            pltpu.VMEM((seq, 2 * HEAD_DIM), BF16),
            pltpu.VMEM((tq, 2 * HEAD_DIM), BF16),
            pltpu.VMEM((tq, 1), F32),
            pltpu.VMEM((tq, 2 * HEAD_DIM), F32),
        ],
        compiler_params=pltpu.CompilerParams(
            dimension_semantics=("parallel", "parallel", "arbitrary"),
            vmem_limit_bytes=_vmem_limit(blocks, scratch, temps)),
        name="fox",
    )(pa, pa, pa, c)


def _proj_out_kernel(hm_ref, hp_ref, ha_ref, w_ref, x_ref, g_ref, o_ref):
    tm = x_ref.shape[0]
    y = jnp.dot(hm_ref[...], w_ref[0:MLSTM_WIDTH, :], preferred_element_type=F32)
    y = y + jnp.dot(hp_ref[...], w_ref[MLSTM_WIDTH:MLSTM_WIDTH + POOL_WIDTH, :],
                    preferred_element_type=F32)
    y = y + jnp.dot(ha_ref[...], w_ref[MLSTM_WIDTH + POOL_WIDTH:, :], preferred_element_type=F32)
    o_ref[...] = y
    _postnorm_residual(x_ref, o_ref, g_ref, o_ref, 1.0)


def _proj_out(hm, hp, ha, w_out, x, g, *, tm):
    t, d = x.shape
    assert t % tm == 0
    blocks = (_nbytes((tm, d), BF16) + _nbytes((d, d), BF16) + 2 * _nbytes((tm, d), F32))
    temps = 2 * _nbytes((tm, d), F32)
    return pl.pallas_call(
        _proj_out_kernel,
        grid=(t // tm,),
        in_specs=[
            pl.BlockSpec((tm, MLSTM_WIDTH), lambda i: (i, 0)),
            pl.BlockSpec((tm, POOL_WIDTH), lambda i: (i, 0)),
            pl.BlockSpec((tm, FOX_WIDTH), lambda i: (i, 0)),
            pl.BlockSpec((d, d), lambda i: (0, 0)),
            pl.BlockSpec((tm, d), lambda i: (i, 0)),
            pl.BlockSpec((1, d), lambda i: (0, 0)),
        ],
        out_specs=pl.BlockSpec((tm, d), lambda i: (i, 0)),
        out_shape=jax.ShapeDtypeStruct((t, d), F32),
        compiler_params=pltpu.CompilerParams(
            dimension_semantics=("parallel",),
            vmem_limit_bytes=_vmem_limit(blocks, 0, temps)),
        name="proj_out",
    )(hm, hp, ha, w_out, x, g.reshape(1, d))


def _mixer(x, pre_g, post_g, w_in_parts, conv_w, b_i, b_f, head_g, pool_w, pool_scale, fox_b_f,
           w_out, *, batch, seq):
    w_pm, w_pa, w_gate = w_in_parts
    bias = jnp.concatenate([b_i, b_f, fox_b_f]).reshape(N_GATES, 1)

    pm, pa, pg = _proj_in(x, pre_g, w_pm, w_pa, w_gate, tm=256)
    grow, gcol = _gates(pg, bias, batch=batch, seq=seq)
    hm = _mlstm(pm, gcol, grow, conv_w, head_g, batch=batch, seq=seq, tb=min(512, seq))
    hp = _pool(pm, pool_w.astype(BF16), pool_scale, batch=batch, seq=seq, ts=min(512, seq))
    ha = _fox(pa, grow, batch=batch, seq=seq, tq=min(2048, seq), n_parts=4)
    return _proj_out(hm, hp, ha, w_out, x, post_g, tm=512)


def kernel(x, ffn1_pre_g, ffn1_post_g, ffn1_w_gate, ffn1_w_up, ffn1_w_down, mix_pre_g, mix_post_g,
           w_in, mlstm_conv, mlstm_b_i, mlstm_b_f, mlstm_head_g, pool_w, pool_scale, fox_b_f, w_out,
           ffn2_pre_g, ffn2_post_g, ffn2_w_gate, ffn2_w_up, ffn2_w_down):
    batch, seq, d = x.shape
    depth = w_in.shape[0]
    xt = x.reshape(batch * seq, d)
    for l in range(depth):
        xt = _ffn(xt, ffn1_pre_g[l], ffn1_post_g[l], _layer_bf16(ffn1_w_gate, l),
                  _layer_bf16(ffn1_w_up, l), _layer_bf16(ffn1_w_down, l), tm=1024, tf=512)
        xt = _mixer(xt, mix_pre_g[l], mix_post_g[l], _split_w_in(w_in, l, tr=256), mlstm_conv[l],
                    mlstm_b_i[l],
                    mlstm_b_f[l], mlstm_head_g[l], pool_w[l], pool_scale[l], fox_b_f[l],
                    _layer_bf16(w_out, l), batch=batch, seq=seq)
        xt = _ffn(xt, ffn2_pre_g[l], ffn2_post_g[l], _layer_bf16(ffn2_w_gate, l),
                  _layer_bf16(ffn2_w_up, l), _layer_bf16(ffn2_w_down, l), tm=1024, tf=512)
    return xt.reshape(batch, seq, d)
```

```python
import functools

import jax
import jax.numpy as jnp
from jax import lax
from jax.experimental import pallas as pl
from jax.experimental.pallas import tpu as pltpu

F32 = jnp.float32
BF16 = jnp.bfloat16

D_MODEL = 2048
HEAD_DIM = 128
MLSTM_WIDTH = D_MODEL // 4
MLSTM_HEADS = MLSTM_WIDTH // HEAD_DIM
POOL_WIDTH = D_MODEL // 4
POOL_WINDOWS = (2, 4, 8, 16)
POOL_GROUP_DIM = POOL_WIDTH // len(POOL_WINDOWS)
FOX_WIDTH = D_MODEL - MLSTM_WIDTH - POOL_WIDTH
FOX_HEADS = FOX_WIDTH // HEAD_DIM
CONV_WIDTH = 4
MLSTM_CHUNK = 128
RMS_EPS = 1e-6
FFN_RESIDUAL_WEIGHT = 0.5

OFF_MI = 4 * MLSTM_WIDTH
OFF_MF = OFF_MI + MLSTM_HEADS
OFF_POOL = OFF_MF + MLSTM_HEADS
OFF_AQ = OFF_POOL + POOL_WIDTH
OFF_AF = OFF_AQ + 3 * FOX_WIDTH
N_IN = OFF_AF + FOX_HEADS

PM_WIDTH = 4 * MLSTM_WIDTH + POOL_WIDTH
PA_WIDTH = 3 * FOX_WIDTH
N_GATES = 2 * MLSTM_HEADS + FOX_HEADS
LOG2E = 1.4426950408889634
FOX_Q_SCALE = LOG2E * HEAD_DIM ** -0.5
LANES = 128
SUBLANES = 8
VMEM_LIMIT_CAP = 58 * 1024 * 1024


def _vmem_limit(block_bytes, scratch_bytes, temp_bytes):
    return int(min(2 * block_bytes + scratch_bytes + temp_bytes + (4 << 20), VMEM_LIMIT_CAP))


def _nbytes(shape, dtype):
    n = 1
    for s in shape:
        n *= s
    return n * jnp.dtype(dtype).itemsize


def _rms_scale(x):
    return lax.rsqrt(jnp.mean(x * x, axis=-1, keepdims=True) + RMS_EPS)


CAST_BLOCK_BYTES = 6 << 20


def _cast_kernel(w_ref, o_ref):
    o_ref[...] = w_ref[...].astype(BF16)


def _layer_bf16(w, layer):
    _, r, c = w.shape
    tr = r
    while _nbytes((tr, c), F32) > CAST_BLOCK_BYTES and tr % 2 == 0 and (tr // 2) % 16 == 0:
        tr //= 2
    assert r % tr == 0
    blocks = _nbytes((tr, c), F32) + _nbytes((tr, c), BF16)
    return pl.pallas_call(
        _cast_kernel,
        grid=(r // tr,),
        in_specs=[pl.BlockSpec((None, tr, c), lambda i: (layer, i, 0))],
        out_specs=pl.BlockSpec((tr, c), lambda i: (i, 0)),
        out_shape=jax.ShapeDtypeStruct((r, c), BF16),
        compiler_params=pltpu.CompilerParams(
            dimension_semantics=("parallel",),
            vmem_limit_bytes=_vmem_limit(blocks, 0, 0)),
        name="cast",
    )(w)


def _split_w_in_kernel(w_ref, pm_ref, pa_ref, gate_ref):
    pm_ref[:, 0:OFF_MI] = w_ref[:, 0:OFF_MI].astype(BF16)
    pm_ref[:, OFF_MI:PM_WIDTH] = w_ref[:, OFF_POOL:OFF_AQ].astype(BF16)
    pa_ref[...] = w_ref[:, OFF_AQ:OFF_AF].astype(BF16)
    gates = jnp.concatenate([w_ref[:, OFF_MI:OFF_POOL], w_ref[:, OFF_AF:N_IN],
                             jnp.zeros((w_ref.shape[0], LANES - N_GATES), F32)], axis=1)
    gate_ref[...] = gates.astype(BF16)


def _split_w_in(w_in, layer, *, tr):
    _, r, c = w_in.shape
    assert r % tr == 0 and c == N_IN
    blocks = _nbytes((tr, c), F32) + _nbytes((tr, PM_WIDTH + PA_WIDTH + LANES), BF16)
    return pl.pallas_call(
        _split_w_in_kernel,
        grid=(r // tr,),
        in_specs=[pl.BlockSpec((None, tr, c), lambda i: (layer, i, 0))],
        out_specs=[
            pl.BlockSpec((tr, PM_WIDTH), lambda i: (i, 0)),
            pl.BlockSpec((tr, PA_WIDTH), lambda i: (i, 0)),
            pl.BlockSpec((tr, LANES), lambda i: (i, 0)),
        ],
        out_shape=[
            jax.ShapeDtypeStruct((r, PM_WIDTH), BF16),
            jax.ShapeDtypeStruct((r, PA_WIDTH), BF16),
            jax.ShapeDtypeStruct((r, LANES), BF16),
        ],
        compiler_params=pltpu.CompilerParams(
            dimension_semantics=("parallel",),
            vmem_limit_bytes=_vmem_limit(blocks, 0, 2 * _nbytes((tr, c), F32))),
        name="split_w_in",
    )(w_in)


NORM_ROWS = 32
NORM_COLS = 512


def _row_rms_scale(ref, rows):
    d = ref.shape[1]
    ss = None
    for c0 in range(0, d, NORM_COLS):
        v = ref[rows, c0:c0 + NORM_COLS]
        ss = v * v if ss is None else ss + v * v
    return lax.rsqrt(jnp.sum(ss, axis=-1, keepdims=True) * (1.0 / d) + RMS_EPS)


def _normalise_rows(n_rows, src_ref, apply_fn):
    groups = 2
    trip_rows = groups * NORM_ROWS
    n_trips = n_rows // trip_rows

    def rows_of(r):
        base = pl.multiple_of(r * trip_rows, trip_rows)
        return [pl.ds(base + k * NORM_ROWS, NORM_ROWS) for k in range(groups)]

    def scales_of(r):
        return tuple(_row_rms_scale(src_ref, rows) for rows in rows_of(r))

    def body(r, scales):
        nxt = scales_of(jnp.minimum(r + 1, n_trips - 1))
        for rows, scale in zip(rows_of(r), scales):
            apply_fn(rows, scale)
        return nxt

    lax.fori_loop(0, n_trips, body, scales_of(0))


def _prenorm(x_ref, g_ref, h_ref):
    def apply(rows, scale):
        for c0 in range(0, x_ref.shape[1], NORM_COLS):
            cols = slice(c0, c0 + NORM_COLS)
            h_ref[rows, cols] = (x_ref[rows, cols] * scale * g_ref[:, cols]).astype(BF16)
    _normalise_rows(x_ref.shape[0], x_ref, apply)


def _postnorm_residual(x_ref, y_ref, g_ref, o_ref, weight):
    def apply(rows, scale):
        scale = scale * weight
        for c0 in range(0, x_ref.shape[1], NORM_COLS):
            cols = slice(c0, c0 + NORM_COLS)
            o_ref[rows, cols] = x_ref[rows, cols] + y_ref[rows, cols] * scale * g_ref[:, cols]
    _normalise_rows(x_ref.shape[0], y_ref, apply)


def _ffn_kernel(x_ref, pre_g_ref, post_g_ref, wg_ref, wu_ref, wd_ref, o_ref, h_ref, *, n_f):
    j = pl.program_id(1)
    tm = x_ref.shape[0]

    @pl.when(j == 0)
    def _():
        _prenorm(x_ref, pre_g_ref, h_ref)
        o_ref[...] = jnp.zeros(o_ref.shape, F32)

    h = h_ref[...]
    g = jnp.dot(h, wg_ref[...], preferred_element_type=F32)
    u = jnp.dot(h, wu_ref[...], preferred_element_type=F32)
    a = (g * jax.nn.sigmoid(g) * u).astype(BF16)
    o_ref[...] += jnp.dot(a, wd_ref[...], preferred_element_type=F32)

    @pl.when(j == n_f - 1)
    def _():
        _postnorm_residual(x_ref, o_ref, post_g_ref, o_ref, FFN_RESIDUAL_WEIGHT)


def _ffn(x, pre_g, post_g, wg, wu, wd, *, tm, tf):
    t, d = x.shape
    f = wg.shape[1]
    assert t % tm == 0 and f % tf == 0 and FFN_RESIDUAL_WEIGHT == 0.5
    blocks = (_nbytes((tm, d), F32) * 2 + 2 * _nbytes((d, tf), BF16) + _nbytes((tf, d), BF16))
    scratch = _nbytes((tm, d), BF16)
    temps = 3 * _nbytes((tm, tf), F32)
    return pl.pallas_call(
        functools.partial(_ffn_kernel, n_f=f // tf),
        grid=(t // tm, f // tf),
        in_specs=[
            pl.BlockSpec((tm, d), lambda i, j: (i, 0)),
            pl.BlockSpec((1, d), lambda i, j: (0, 0)),
            pl.BlockSpec((1, d), lambda i, j: (0, 0)),
            pl.BlockSpec((d, tf), lambda i, j: (0, j)),
            pl.BlockSpec((d, tf), lambda i, j: (0, j)),
            pl.BlockSpec((tf, d), lambda i, j: (j, 0)),
        ],
        out_specs=pl.BlockSpec((tm, d), lambda i, j: (i, 0)),
        out_shape=jax.ShapeDtypeStruct((t, d), F32),
        scratch_shapes=[pltpu.VMEM((tm, d), BF16)],
        compiler_params=pltpu.CompilerParams(
            dimension_semantics=("parallel", "arbitrary"),
            vmem_limit_bytes=_vmem_limit(blocks, scratch, temps)),
        name="ffn",
    )(x, pre_g.reshape(1, d), post_g.reshape(1, d), wg, wu, wd)


def _proj_in_kernel(x_ref, g_ref, wpm_ref, wpa_ref, wgate_ref, pm_ref, pa_ref, pg_ref):
    x = x_ref[...]
    h = (x * _rms_scale(x) * g_ref[...]).astype(BF16)
    pm_ref[...] = jnp.dot(h, wpm_ref[...], preferred_element_type=F32)
    pa = jnp.dot(h, wpa_ref[...], preferred_element_type=F32)
    pa_ref[:, 0:FOX_WIDTH] = (pa[:, 0:FOX_WIDTH] * FOX_Q_SCALE).astype(BF16)
    pa_ref[:, FOX_WIDTH:] = pa[:, FOX_WIDTH:].astype(BF16)
    pg_ref[...] = jnp.dot(h, wgate_ref[...], preferred_element_type=F32)


def _proj_in(x, g, w_pm, w_pa, w_gate, *, tm):
    t, d = x.shape
    assert t % tm == 0 and w_pm.shape == (d, PM_WIDTH) and w_pa.shape == (d, PA_WIDTH)
    resident = pl.Buffered(1)
    blocks = (_nbytes((tm, d), F32) + _nbytes((d, LANES), BF16) + _nbytes((tm, PM_WIDTH), F32)
              + _nbytes((tm, PA_WIDTH), BF16) + _nbytes((tm, LANES), F32))
    scratch = _nbytes((d, PM_WIDTH + PA_WIDTH), BF16)
    temps = _nbytes((tm, d), BF16) + 2 * _nbytes((tm, PA_WIDTH), F32)
    return pl.pallas_call(
        _proj_in_kernel,
        grid=(t // tm,),
        in_specs=[
            pl.BlockSpec((tm, d), lambda i: (i, 0)),
            pl.BlockSpec((1, d), lambda i: (0, 0)),
            pl.BlockSpec((d, PM_WIDTH), lambda i: (0, 0), pipeline_mode=resident),
            pl.BlockSpec((d, PA_WIDTH), lambda i: (0, 0), pipeline_mode=resident),
            pl.BlockSpec((d, LANES), lambda i: (0, 0)),
        ],
        out_specs=[
            pl.BlockSpec((tm, PM_WIDTH), lambda i: (i, 0)),
            pl.BlockSpec((tm, PA_WIDTH), lambda i: (i, 0)),
            pl.BlockSpec((tm, LANES), lambda i: (i, 0)),
        ],
        out_shape=[
            jax.ShapeDtypeStruct((t, PM_WIDTH), F32),
            jax.ShapeDtypeStruct((t, PA_WIDTH), BF16),
            jax.ShapeDtypeStruct((t, LANES), F32),
        ],
        compiler_params=pltpu.CompilerParams(
            dimension_semantics=("parallel",),
            vmem_limit_bytes=_vmem_limit(blocks, scratch, temps)),
        name="proj_in",
    )(x, g.reshape(1, d), w_pm, w_pa, w_gate)


def _log_sigmoid(z):
    return -(jnp.maximum(-z, 0.0) + jnp.log1p(jnp.exp(-jnp.abs(z))))


def _lane_scan(v, shifts, lane_pos):
    for sh in shifts:
        v = v + jnp.where(lane_pos >= sh, pltpu.roll(v, sh, axis=1), 0.0)
    return v


def _gates_kernel(pg_ref, bias_ref, row_ref, col_ref):
    s = pg_ref.shape[0]
    z = pg_ref[...].T[0:N_GATES, :] + bias_ref[...]
    lane = lax.broadcasted_iota(jnp.int32, (1, s), 1)
    mi = z[0:MLSTM_HEADS]
    lf_m = _log_sigmoid(z[MLSTM_HEADS:2 * MLSTM_HEADS])
    lf_a = _log_sigmoid(z[2 * MLSTM_HEADS:N_GATES])
    chunk_shifts = [1 << k for k in range(MLSTM_CHUNK.bit_length() - 1)]
    seq_shifts = [1 << k for k in range((s - 1).bit_length())]
    bcum = _lane_scan(lf_m, chunk_shifts, lane % MLSTM_CHUNK)
    cfox = _lane_scan(lf_a, seq_shifts, lane)
    rows = jnp.concatenate([mi, bcum, cfox], axis=0)
    row_ref[0] = rows
    padded = jnp.concatenate([rows, jnp.zeros((LANES - N_GATES, s), F32)], axis=0)
    col_ref[...] = padded.T


def _gates(pg, bias, *, batch, seq):
    t = pg.shape[0]
    assert t == batch * seq
    blocks = 2 * _nbytes((seq, LANES), F32) + _nbytes((N_GATES, seq), F32)
    temps = 6 * _nbytes((seq, LANES), F32)
    return pl.pallas_call(
        _gates_kernel,
        grid=(batch,),
        in_specs=[
            pl.BlockSpec((seq, LANES), lambda b: (b, 0)),
            pl.BlockSpec((N_GATES, 1), lambda b: (0, 0)),
        ],
        out_specs=[
            pl.BlockSpec((1, N_GATES, seq), lambda b: (b, 0, 0)),
            pl.BlockSpec((seq, LANES), lambda b: (b, 0)),
        ],
        out_shape=[
            jax.ShapeDtypeStruct((batch, N_GATES, seq), F32),
            jax.ShapeDtypeStruct((t, LANES), F32),
        ],
        compiler_params=pltpu.CompilerParams(
            dimension_semantics=("parallel",),
            vmem_limit_bytes=_vmem_limit(blocks, 0, temps)),
        name="gates",
    )(pg, bias)


def _mlstm_kernel(qk_ref, v_ref, o_ref, gcol_ref, grow_ref, convw_ref, hg_ref, out_ref,
                  ubuf, qkc, cn_ref, m_ref, *, tb):
    step = pl.program_id(1)
    L = MLSTM_CHUNK
    halo = SUBLANES

    @pl.when(step == 0)
    def _():
        ubuf[0:halo, :] = jnp.zeros((halo, 2 * MLSTM_WIDTH), F32)
        cn_ref[...] = jnp.zeros(cn_ref.shape, F32)
        m_ref[...] = jnp.zeros(m_ref.shape, F32)

    @pl.when(step > 0)
    def _():
        ubuf[0:halo, :] = ubuf[tb:tb + halo, :]

    ubuf[halo:halo + tb, :] = qk_ref[...]
    w = convw_ref[...]
    y = ubuf[halo:halo + tb, :] * w[0:1, :]
    for j in range(1, CONV_WIDTH):
        y = y + ubuf[halo - j:halo - j + tb, :] * w[j:j + 1, :]
    qkc[...] = y * jax.nn.sigmoid(y)

    tri = (lax.broadcasted_iota(jnp.int32, (L, L), 0) >= lax.broadcasted_iota(jnp.int32, (L, L), 1))
    ones_col = (lax.broadcasted_iota(jnp.int32, (L, HEAD_DIM), 1) == 0).astype(F32)
    q_scale = HEAD_DIM ** -0.5

    for c in range(tb // L):
        rows = slice(c * L, (c + 1) * L)
        for hd in range(MLSTM_HEADS):
            hs = slice(hd * HEAD_DIM, (hd + 1) * HEAD_DIM)
            ks = slice(MLSTM_WIDTH + hd * HEAD_DIM, MLSTM_WIDTH + (hd + 1) * HEAD_DIM)
            q = (qkc[rows, hs] * q_scale).astype(BF16)
            k = qkc[rows, ks]
            k_bf = k.astype(BF16)
            vaug = jnp.concatenate([v_ref[rows, hs], ones_col], axis=1)
            i_col = gcol_ref[rows, hd:hd + 1]
            b_col = gcol_ref[rows, MLSTM_HEADS + hd:MLSTM_HEADS + hd + 1]
            i_row = grow_ref[0, hd:hd + 1, rows]
            b_row = grow_ref[0, MLSTM_HEADS + hd:MLSTM_HEADS + hd + 1, rows]
            m_prev = m_ref[hd][0:1, 0:1]
            cn = cn_ref[hd]

            log_d = jnp.where(tri, b_col - b_row + i_row, -jnp.inf)
            inter = b_col + m_prev
            m_t = jnp.maximum(inter, jnp.max(log_d, axis=-1, keepdims=True))
            s = lax.dot_general(q, k_bf, (((1,), (1,)), ((), ())), preferred_element_type=F32)
            scores = (s * jnp.exp(log_d - m_t)).astype(BF16)
            inter_w = jnp.exp(inter - m_t)
            numden = (jnp.dot(scores, vaug.astype(BF16), preferred_element_type=F32)
                      + inter_w * jnp.dot(q, cn.astype(BF16), preferred_element_type=F32))
            num = numden[:, 0:HEAD_DIM]
            den = numden[:, HEAD_DIM:HEAD_DIM + 1]
            h = num / jnp.maximum(jnp.abs(den), jnp.exp(-m_t))

            b_last = b_row[:, L - 1:L]
            log_w = b_last - b_col + i_col
            m_new = jnp.maximum(b_last + m_prev, jnp.max(log_w, axis=0, keepdims=True))
            wv = (jnp.exp(log_w - m_new) * vaug).astype(BF16)
            decay = jnp.exp(b_last + m_prev - m_new)
            cn_ref[hd] = decay * cn + jnp.dot(k.T.astype(BF16), wv, preferred_element_type=F32)
            m_ref[hd] = jnp.broadcast_to(m_new, (SUBLANES, LANES))

            hn = h * _rms_scale(h)
            out_ref[rows, hs] = (hn * hg_ref[:, hs] * jax.nn.sigmoid(o_ref[rows, hs])).astype(BF16)


def _mlstm(pm, gcol, grow, conv_w, head_g, *, batch, seq, tb):
    t = pm.shape[0]
    assert seq % tb == 0 and tb % MLSTM_CHUNK == 0
    nb = seq // tb
    w2 = 2 * MLSTM_WIDTH
    blocks = (_nbytes((tb, w2), F32) + 2 * _nbytes((tb, MLSTM_WIDTH), F32) + _nbytes((tb, LANES), F32)
              + _nbytes((N_GATES, tb), F32) + _nbytes((tb, MLSTM_WIDTH), BF16))
    scratch = (_nbytes((tb + 2 * SUBLANES, w2), F32) + _nbytes((tb, w2), F32)
               + _nbytes((MLSTM_HEADS, HEAD_DIM, 2 * HEAD_DIM), F32))
    temps = 2 * _nbytes((tb, w2), F32)
    return pl.pallas_call(
        functools.partial(_mlstm_kernel, tb=tb),
        grid=(batch, nb),
        in_specs=[
            pl.BlockSpec((tb, w2), lambda b, i: (b * nb + i, 0)),
            pl.BlockSpec((tb, MLSTM_WIDTH), lambda b, i: (b * nb + i, 2)),
            pl.BlockSpec((tb, MLSTM_WIDTH), lambda b, i: (b * nb + i, 3)),
            pl.BlockSpec((tb, LANES), lambda b, i: (b * nb + i, 0)),
            pl.BlockSpec((1, N_GATES, tb), lambda b, i: (b, 0, i)),
            pl.BlockSpec((CONV_WIDTH, w2), lambda b, i: (0, 0)),
            pl.BlockSpec((1, MLSTM_WIDTH), lambda b, i: (0, 0)),
        ],
        out_specs=pl.BlockSpec((tb, MLSTM_WIDTH), lambda b, i: (b * nb + i, 0)),
        out_shape=jax.ShapeDtypeStruct((t, MLSTM_WIDTH), BF16),
        scratch_shapes=[
            pltpu.VMEM((tb + 2 * SUBLANES, w2), F32),
            pltpu.VMEM((tb, w2), F32),
            pltpu.VMEM((MLSTM_HEADS, HEAD_DIM, 2 * HEAD_DIM), F32),
            pltpu.VMEM((MLSTM_HEADS, SUBLANES, LANES), F32),
        ],
        compiler_params=pltpu.CompilerParams(
            dimension_semantics=("parallel", "arbitrary"),
            vmem_limit_bytes=_vmem_limit(blocks, scratch, temps)),
        name="mlstm",
    )(pm, pm, pm, gcol, grow, conv_w, head_g.reshape(1, MLSTM_WIDTH))


def _pool_kernel(u_ref, w_ref, scale_ref, out_ref, ext, *, ts):
    step = pl.program_id(1)
    halo = 2 * SUBLANES
    assert halo >= max(POOL_WINDOWS)

    @pl.when(step == 0)
    def _():
        ext[0:halo, :] = jnp.zeros((halo, POOL_WIDTH), F32)

    @pl.when(step > 0)
    def _():
        ext[0:halo, :] = ext[ts:ts + halo, :]

    ext[halo:halo + ts, :] = u_ref[...]
    pos = (step * ts + lax.broadcasted_iota(jnp.int32, (ts, 1), 0) + 1).astype(F32)
    for g, win in enumerate(POOL_WINDOWS):
        cols = slice(g * POOL_GROUP_DIM, (g + 1) * POOL_GROUP_DIM)
        acc = ext[halo:halo + ts, cols]
        for j in range(1, win):
            acc = acc + ext[halo - j:halo - j + ts, cols]
        mean = acc / jnp.minimum(pos, float(win))
        diff = (mean - ext[halo:halo + ts, cols]).astype(BF16)
        y = jnp.dot(diff, w_ref[g], preferred_element_type=F32)
        out_ref[:, cols] = (y * scale_ref[:, cols]).astype(BF16)


def _pool(pm, pool_w, pool_scale, *, batch, seq, ts):
    t = pm.shape[0]
    assert seq % ts == 0
    nb = seq // ts
    col_block = (4 * MLSTM_WIDTH) // POOL_WIDTH
    blocks = (_nbytes((ts, POOL_WIDTH), F32) + _nbytes(pool_w.shape, BF16)
              + _nbytes((ts, POOL_WIDTH), BF16))
    scratch = _nbytes((ts + 2 * SUBLANES, POOL_WIDTH), F32)
    temps = 4 * _nbytes((ts, POOL_GROUP_DIM), F32)
    return pl.pallas_call(
        functools.partial(_pool_kernel, ts=ts),
        grid=(batch, nb),
        in_specs=[
            pl.BlockSpec((ts, POOL_WIDTH), lambda b, i: (b * nb + i, col_block)),
            pl.BlockSpec(pool_w.shape, lambda b, i: (0, 0, 0)),
            pl.BlockSpec((1, POOL_WIDTH), lambda b, i: (0, 0)),
        ],
        out_specs=pl.BlockSpec((ts, POOL_WIDTH), lambda b, i: (b * nb + i, 0)),
        out_shape=jax.ShapeDtypeStruct((t, POOL_WIDTH), BF16),
        scratch_shapes=[pltpu.VMEM((ts + 2 * SUBLANES, POOL_WIDTH), F32)],
        compiler_params=pltpu.CompilerParams(
            dimension_semantics=("parallel", "arbitrary"),
            vmem_limit_bytes=_vmem_limit(blocks, scratch, temps)),
        name="pool",
    )(pm, pool_w, pool_scale.reshape(1, POOL_WIDTH))


def _split3_bf16(c):
    hi = c.astype(BF16).astype(F32)
    r = c - hi
    mid = r.astype(BF16).astype(F32)
    lo = (r - mid).astype(BF16).astype(F32)
    return hi, mid, lo


def _bias_rows(c_row, n, *, query_side):
    hi, mid, lo = _split3_bf16(c_row)
    ridx = lax.broadcasted_iota(jnp.int32, (HEAD_DIM, n), 0)
    sign = 1.0 if query_side else -1.0
    first = 0 if query_side else 3
    ones_at = 3 if query_side else 0
    rows = jnp.where(ridx == first, sign * hi, 0.0)
    rows = jnp.where(ridx == first + 1, sign * mid, rows)
    rows = jnp.where(ridx == first + 2, sign * lo, rows)
    return jnp.where((ridx >= ones_at) & (ridx < ones_at + 3), 1.0, rows)


def _fox_kernel(q_ref, k_ref, v_ref, c_ref, out_ref, kt_ref, va_ref, qa_ref, m_ref, acc_ref,
                *, tq, nq, n_parts, n_diag_parts):
    qi = pl.program_id(2)

    @pl.when(qi == 0)
    def _():
        ones_col = (lax.broadcasted_iota(jnp.int32, (tq, HEAD_DIM), 1) == 0).astype(BF16)
        for jb in range(nq):
            rows = slice(jb * tq, (jb + 1) * tq)
            kt = k_ref[rows, :].astype(F32).T
            ext = _bias_rows(c_ref[0, jb:jb + 1, :] * LOG2E, tq, query_side=False)
            kt_ref[jb] = jnp.concatenate([kt, ext], axis=0).astype(BF16)
            va_ref[rows, 0:HEAD_DIM] = v_ref[rows, :]
            va_ref[rows, HEAD_DIM:] = ones_col

    c2_q = c_ref[0, pl.ds(qi, 1), :] * LOG2E
    qa_ref[:, 0:HEAD_DIM] = q_ref[...]
    qa_ref[:, HEAD_DIM:] = _bias_rows(c2_q, tq, query_side=True).T.astype(BF16)
    m_ref[...] = jnp.full(m_ref.shape, -jnp.inf, F32)
    acc_ref[...] = jnp.zeros(acc_ref.shape, F32)

    def logits(r0, nr, kv_blk, nc, masked):
        s = jnp.dot(qa_ref[r0:r0 + nr, :], kt_ref[kv_blk][:, 0:nc], preferred_element_type=F32)
        if masked:
            keep = (lax.broadcasted_iota(jnp.int32, (nr, nc), 1)
                    <= lax.broadcasted_iota(jnp.int32, (nr, nc), 0) + r0)
            s = jnp.where(keep, s, -jnp.inf)
        return s

    def accumulate(s, r0, nr, kv_blk, nc):
        m_prev = m_ref[r0:r0 + nr, :]
        m_new = jnp.maximum(m_prev, jnp.max(s, axis=-1, keepdims=True))
        alpha = jnp.exp2(m_prev - m_new)
        p = jnp.exp2(s - m_new).astype(BF16)
        v_rows = pl.ds(pl.multiple_of(kv_blk * tq, tq), nc)
        acc_ref[r0:r0 + nr, :] = (alpha * acc_ref[r0:r0 + nr, :]
                                  + jnp.dot(p, va_ref[v_rows, :], preferred_element_type=F32))
        m_ref[r0:r0 + nr, :] = m_new

    def update(parts, kv_blk, masked):
        ss = [logits(r0, nr, kv_blk, nc, masked) for r0, nr, nc in parts]
        for s, (r0, nr, nc) in zip(ss, parts):
            accumulate(s, r0, nr, kv_blk, nc)

    pr = tq // n_parts

    def body(j, carry):
        update([(i * pr, pr, tq) for i in range(n_parts)], j, False)
        return carry

    lax.fori_loop(0, qi, body, 0)
    dr = tq // n_diag_parts
    update([(i * dr, dr, (i + 1) * dr) for i in range(n_diag_parts)], qi, True)
    out_ref[...] = (acc_ref[:, 0:HEAD_DIM] / acc_ref[:, HEAD_DIM:HEAD_DIM + 1]).astype(BF16)


def _fox(pa, grow, *, batch, seq, tq, n_parts, n_diag_parts):
    t = pa.shape[0]
    assert seq % tq == 0
    nq = seq // tq
    assert tq % n_parts == 0 and (tq // n_parts) % LANES == 0
    assert tq % n_diag_parts == 0 and (tq // n_diag_parts) % LANES == 0
    c = grow.reshape(batch * N_GATES, nq, tq)
    gate0 = 2 * MLSTM_HEADS
    blocks = (2 * _nbytes((tq, HEAD_DIM), BF16) + 2 * _nbytes((seq, HEAD_DIM), BF16)
              + _nbytes((max(nq, SUBLANES), tq), F32))
    scratch = (2 * _nbytes((seq, 2 * HEAD_DIM), BF16) + _nbytes((tq, 2 * HEAD_DIM), BF16)
               + _nbytes((tq, LANES), F32) + _nbytes((tq, 2 * HEAD_DIM), F32))
    temps = 3 * _nbytes((tq, tq), F32)
    return pl.pallas_call(
        functools.partial(_fox_kernel, tq=tq, nq=nq, n_parts=n_parts, n_diag_parts=n_diag_parts),
        grid=(batch, FOX_HEADS, nq),
        in_specs=[
            pl.BlockSpec((tq, HEAD_DIM), lambda b, h, i: (b * nq + i, h)),
            pl.BlockSpec((seq, HEAD_DIM), lambda b, h, i: (b, FOX_HEADS + h)),
            pl.BlockSpec((seq, HEAD_DIM), lambda b, h, i: (b, 2 * FOX_HEADS + h)),
            pl.BlockSpec((1, nq, tq), lambda b, h, i: (b * N_GATES + gate0 + h, 0, 0)),
        ],
        out_specs=pl.BlockSpec((tq, HEAD_DIM), lambda b, h, i: (b * nq + i, h)),
        out_shape=jax.ShapeDtypeStruct((t, FOX_WIDTH), BF16),
        scratch_shapes=[
            pltpu.VMEM((nq, 2 * HEAD_DIM, tq), BF16),
            pltpu.VMEM((seq, 2 * HEAD_DIM), BF16),
            pltpu.VMEM((tq, 2 * HEAD_DIM), BF16),
            pltpu.VMEM((tq, 1), F32),
            pltpu.VMEM((tq, 2 * HEAD_DIM), F32),
        ],
        compiler_params=pltpu.CompilerParams(
            dimension_semantics=("parallel", "parallel", "arbitrary"),
            vmem_limit_bytes=_vmem_limit(blocks, scratch, temps)),
        name="fox",
    )(pa, pa, pa, c)


def _proj_out_kernel(hm_ref, hp_ref, ha_ref, w_ref, x_ref, g_ref, o_ref):
    tm = x_ref.shape[0]
    y = jnp.dot(hm_ref[...], w_ref[0:MLSTM_WIDTH, :], preferred_element_type=F32)
    y = y + jnp.dot(hp_ref[...], w_ref[MLSTM_WIDTH:MLSTM_WIDTH + POOL_WIDTH, :],
                    preferred_element_type=F32)
    y = y + jnp.dot(ha_ref[...], w_ref[MLSTM_WIDTH + POOL_WIDTH:, :], preferred_element_type=F32)
    o_ref[...] = y
    _postnorm_residual(x_ref, o_ref, g_ref, o_ref, 1.0)


def _proj_out(hm, hp, ha, w_out, x, g, *, tm):
    t, d = x.shape
    assert t % tm == 0
    blocks = (_nbytes((tm, d), BF16) + _nbytes((d, d), BF16) + 2 * _nbytes((tm, d), F32))
    temps = 2 * _nbytes((tm, d), F32)
    return pl.pallas_call(
        _proj_out_kernel,
        grid=(t // tm,),
        in_specs=[
            pl.BlockSpec((tm, MLSTM_WIDTH), lambda i: (i, 0)),
            pl.BlockSpec((tm, POOL_WIDTH), lambda i: (i, 0)),
            pl.BlockSpec((tm, FOX_WIDTH), lambda i: (i, 0)),
            pl.BlockSpec((d, d), lambda i: (0, 0)),
            pl.BlockSpec((tm, d), lambda i: (i, 0)),
            pl.BlockSpec((1, d), lambda i: (0, 0)),
        ],
        out_specs=pl.BlockSpec((tm, d), lambda i: (i, 0)),
        out_shape=jax.ShapeDtypeStruct((t, d), F32),
        compiler_params=pltpu.CompilerParams(
            dimension_semantics=("parallel",),
            vmem_limit_bytes=_vmem_limit(blocks, 0, temps)),
        name="proj_out",
    )(hm, hp, ha, w_out, x, g.reshape(1, d))


def _mixer(x, pre_g, post_g, w_in_parts, conv_w, b_i, b_f, head_g, pool_w, pool_scale, fox_b_f,
           w_out, *, batch, seq):
    w_pm, w_pa, w_gate = w_in_parts
    bias = jnp.concatenate([b_i, b_f, fox_b_f]).reshape(N_GATES, 1)

    pm, pa, pg = _proj_in(x, pre_g, w_pm, w_pa, w_gate, tm=512)
    grow, gcol = _gates(pg, bias, batch=batch, seq=seq)
    hm = _mlstm(pm, gcol, grow, conv_w, head_g, batch=batch, seq=seq, tb=min(512, seq))
    hp = _pool(pm, pool_w.astype(BF16), pool_scale, batch=batch, seq=seq, ts=min(512, seq))
    ha = _fox(pa, grow, batch=batch, seq=seq, tq=min(2048, seq), n_parts=4, n_diag_parts=8)
    return _proj_out(hm, hp, ha, w_out, x, post_g, tm=512)


def kernel(x, ffn1_pre_g, ffn1_post_g, ffn1_w_gate, ffn1_w_up, ffn1_w_down, mix_pre_g, mix_post_g,
           w_in, mlstm_conv, mlstm_b_i, mlstm_b_f, mlstm_head_g, pool_w, pool_scale, fox_b_f, w_out,
           ffn2_pre_g, ffn2_post_g, ffn2_w_gate, ffn2_w_up, ffn2_w_down):
    batch, seq, d = x.shape
    depth = w_in.shape[0]
    xt = x.reshape(batch * seq, d)
    for l in range(depth):
        xt = _ffn(xt, ffn1_pre_g[l], ffn1_post_g[l], _layer_bf16(ffn1_w_gate, l),
                  _layer_bf16(ffn1_w_up, l), _layer_bf16(ffn1_w_down, l), tm=1024, tf=512)
        xt = _mixer(xt, mix_pre_g[l], mix_post_g[l], _split_w_in(w_in, l, tr=256), mlstm_conv[l],
                    mlstm_b_i[l],
                    mlstm_b_f[l], mlstm_head_g[l], pool_w[l], pool_scale[l], fox_b_f[l],
                    _layer_bf16(w_out, l), batch=batch, seq=seq)
        xt = _ffn(xt, ffn2_pre_g[l], ffn2_post_g[l], _layer_bf16(ffn2_w_gate, l),
                  _layer_bf16(ffn2_w_up, l), _layer_bf16(ffn2_w_down, l), tm=1024, tf=512)
    return xt.reshape(batch, seq, d)
```

```python
import functools

import jax
import jax.numpy as jnp
from jax import lax
from jax.experimental import pallas as pl
from jax.experimental.pallas import tpu as pltpu

F32 = jnp.float32
BF16 = jnp.bfloat16

D_MODEL = 2048
HEAD_DIM = 128
MLSTM_WIDTH = D_MODEL // 4
MLSTM_HEADS = MLSTM_WIDTH // HEAD_DIM
POOL_WIDTH = D_MODEL // 4
POOL_WINDOWS = (2, 4, 8, 16)
POOL_GROUP_DIM = POOL_WIDTH // len(POOL_WINDOWS)
FOX_WIDTH = D_MODEL - MLSTM_WIDTH - POOL_WIDTH
FOX_HEADS = FOX_WIDTH // HEAD_DIM
CONV_WIDTH = 4
MLSTM_CHUNK = 128
RMS_EPS = 1e-6
FFN_RESIDUAL_WEIGHT = 0.5

OFF_MI = 4 * MLSTM_WIDTH
OFF_MF = OFF_MI + MLSTM_HEADS
OFF_POOL = OFF_MF + MLSTM_HEADS
OFF_AQ = OFF_POOL + POOL_WIDTH
OFF_AF = OFF_AQ + 3 * FOX_WIDTH
N_IN = OFF_AF + FOX_HEADS

PM_WIDTH = 4 * MLSTM_WIDTH + POOL_WIDTH
PA_WIDTH = 3 * FOX_WIDTH
N_GATES = 2 * MLSTM_HEADS + FOX_HEADS
LOG2E = 1.4426950408889634
FOX_Q_SCALE = LOG2E * HEAD_DIM ** -0.5
LANES = 128
SUBLANES = 8
VMEM_LIMIT_CAP = 58 * 1024 * 1024


def _vmem_limit(block_bytes, scratch_bytes, temp_bytes):
    return int(min(2 * block_bytes + scratch_bytes + temp_bytes + (4 << 20), VMEM_LIMIT_CAP))


def _nbytes(shape, dtype):
    n = 1
    for s in shape:
        n *= s
    return n * jnp.dtype(dtype).itemsize


def _rms_scale(x):
    return lax.rsqrt(jnp.mean(x * x, axis=-1, keepdims=True) + RMS_EPS)


CAST_BLOCK_BYTES = 6 << 20


def _cast_kernel(w_ref, o_ref):
    o_ref[...] = w_ref[...].astype(BF16)


def _layer_bf16(w, layer):
    _, r, c = w.shape
    tr = r
    while _nbytes((tr, c), F32) > CAST_BLOCK_BYTES and tr % 2 == 0 and (tr // 2) % 16 == 0:
        tr //= 2
    assert r % tr == 0
    blocks = _nbytes((tr, c), F32) + _nbytes((tr, c), BF16)
    return pl.pallas_call(
        _cast_kernel,
        grid=(r // tr,),
        in_specs=[pl.BlockSpec((None, tr, c), lambda i: (layer, i, 0))],
        out_specs=pl.BlockSpec((tr, c), lambda i: (i, 0)),
        out_shape=jax.ShapeDtypeStruct((r, c), BF16),
        compiler_params=pltpu.CompilerParams(
            dimension_semantics=("parallel",),
            vmem_limit_bytes=_vmem_limit(blocks, 0, 0)),
        name="cast",
    )(w)


def _split_w_in_kernel(wt_ref, pm_ref, pa_ref, gate_ref):
    tr = wt_ref.shape[1]
    pm_ref[:, 0:OFF_MI] = wt_ref[0:OFF_MI, :].T.astype(BF16)
    pm_ref[:, OFF_MI:PM_WIDTH] = wt_ref[OFF_POOL:OFF_AQ, :].T.astype(BF16)
    pa_ref[...] = wt_ref[OFF_AQ:OFF_AF, :].T.astype(BF16)
    gates = jnp.concatenate([wt_ref[OFF_MI:OFF_POOL, :], wt_ref[OFF_AF:N_IN, :],
                             jnp.zeros((LANES - N_GATES, tr), F32)], axis=0)
    gate_ref[...] = gates.T.astype(BF16)


def _split_w_in(w_in, layer, *, tr):
    _, r, c = w_in.shape
    assert r % tr == 0 and c == N_IN and tr % LANES == 0
    assert all(off % SUBLANES == 0 for off in (OFF_MI, OFF_POOL, OFF_AQ, OFF_AF))
    blocks = _nbytes((c, tr), F32) + _nbytes((tr, PM_WIDTH + PA_WIDTH + LANES), BF16)
    return pl.pallas_call(
        _split_w_in_kernel,
        grid=(r // tr,),
        in_specs=[pl.BlockSpec((None, c, tr), lambda i: (layer, 0, i))],
        out_specs=[
            pl.BlockSpec((tr, PM_WIDTH), lambda i: (i, 0)),
            pl.BlockSpec((tr, PA_WIDTH), lambda i: (i, 0)),
            pl.BlockSpec((tr, LANES), lambda i: (i, 0)),
        ],
        out_shape=[
            jax.ShapeDtypeStruct((r, PM_WIDTH), BF16),
            jax.ShapeDtypeStruct((r, PA_WIDTH), BF16),
            jax.ShapeDtypeStruct((r, LANES), BF16),
        ],
        compiler_params=pltpu.CompilerParams(
            dimension_semantics=("parallel",),
            vmem_limit_bytes=_vmem_limit(blocks, 0, 2 * _nbytes((tr, c), F32))),
        name="split_w_in",
    )(jnp.swapaxes(w_in, 1, 2))


NORM_ROWS = 32
NORM_COLS = 512


def _row_rms_scale(ref, rows):
    d = ref.shape[1]
    ss = None
    for c0 in range(0, d, NORM_COLS):
        v = ref[rows, c0:c0 + NORM_COLS]
        ss = v * v if ss is None else ss + v * v
    return lax.rsqrt(jnp.sum(ss, axis=-1, keepdims=True) * (1.0 / d) + RMS_EPS)


def _normalise_rows(n_rows, src_ref, apply_fn):
    groups = 2
    trip_rows = groups * NORM_ROWS
    n_trips = n_rows // trip_rows

    def rows_of(r):
        base = pl.multiple_of(r * trip_rows, trip_rows)
        return [pl.ds(base + k * NORM_ROWS, NORM_ROWS) for k in range(groups)]

    def scales_of(r):
        return tuple(_row_rms_scale(src_ref, rows) for rows in rows_of(r))

    def body(r, scales):
        nxt = scales_of(jnp.minimum(r + 1, n_trips - 1))
        for rows, scale in zip(rows_of(r), scales):
            apply_fn(rows, scale)
        return nxt

    lax.fori_loop(0, n_trips, body, scales_of(0))


def _prenorm(x_ref, g_ref, h_ref):
    def apply(rows, scale):
        for c0 in range(0, x_ref.shape[1], NORM_COLS):
            cols = slice(c0, c0 + NORM_COLS)
            h_ref[rows, cols] = (x_ref[rows, cols] * scale * g_ref[:, cols]).astype(BF16)
    _normalise_rows(x_ref.shape[0], x_ref, apply)


def _postnorm_residual(x_ref, y_ref, g_ref, o_ref, weight):
    def apply(rows, scale):
        scale = scale * weight
        for c0 in range(0, x_ref.shape[1], NORM_COLS):
            cols = slice(c0, c0 + NORM_COLS)
            o_ref[rows, cols] = x_ref[rows, cols] + y_ref[rows, cols] * scale * g_ref[:, cols]
    _normalise_rows(x_ref.shape[0], y_ref, apply)


def _ffn_kernel(x_ref, pre_g_ref, post_g_ref, wg_ref, wu_ref, wd_ref, o_ref, h_ref, *, n_f):
    j = pl.program_id(1)
    tm = x_ref.shape[0]

    @pl.when(j == 0)
    def _():
        _prenorm(x_ref, pre_g_ref, h_ref)
        o_ref[...] = jnp.zeros(o_ref.shape, F32)

    h = h_ref[...]
    g = jnp.dot(h, wg_ref[...], preferred_element_type=F32)
    u = jnp.dot(h, wu_ref[...], preferred_element_type=F32)
    a = (g * jax.nn.sigmoid(g) * u).astype(BF16)
    o_ref[...] += jnp.dot(a, wd_ref[...], preferred_element_type=F32)

    @pl.when(j == n_f - 1)
    def _():
        _postnorm_residual(x_ref, o_ref, post_g_ref, o_ref, FFN_RESIDUAL_WEIGHT)


def _ffn(x, pre_g, post_g, wg, wu, wd, *, tm, tf):
    t, d = x.shape
    f = wg.shape[1]
    assert t % tm == 0 and f % tf == 0 and FFN_RESIDUAL_WEIGHT == 0.5
    blocks = (_nbytes((tm, d), F32) * 2 + 2 * _nbytes((d, tf), BF16) + _nbytes((tf, d), BF16))
    scratch = _nbytes((tm, d), BF16)
    temps = 3 * _nbytes((tm, tf), F32)
    return pl.pallas_call(
        functools.partial(_ffn_kernel, n_f=f // tf),
        grid=(t // tm, f // tf),
        in_specs=[
            pl.BlockSpec((tm, d), lambda i, j: (i, 0)),
            pl.BlockSpec((1, d), lambda i, j: (0, 0)),
            pl.BlockSpec((1, d), lambda i, j: (0, 0)),
            pl.BlockSpec((d, tf), lambda i, j: (0, j)),
            pl.BlockSpec((d, tf), lambda i, j: (0, j)),
            pl.BlockSpec((tf, d), lambda i, j: (j, 0)),
        ],
        out_specs=pl.BlockSpec((tm, d), lambda i, j: (i, 0)),
        out_shape=jax.ShapeDtypeStruct((t, d), F32),
        scratch_shapes=[pltpu.VMEM((tm, d), BF16)],
        compiler_params=pltpu.CompilerParams(
            dimension_semantics=("parallel", "arbitrary"),
            vmem_limit_bytes=_vmem_limit(blocks, scratch, temps)),
        name="ffn",
    )(x, pre_g.reshape(1, d), post_g.reshape(1, d), wg, wu, wd)


def _proj_in_kernel(x_ref, g_ref, wpm_ref, wpa_ref, wgate_ref, pm_ref, pa_ref, pg_ref):
    x = x_ref[...]
    h = (x * _rms_scale(x) * g_ref[...]).astype(BF16)
    pm_ref[...] = jnp.dot(h, wpm_ref[...], preferred_element_type=F32)
    pa = jnp.dot(h, wpa_ref[...], preferred_element_type=F32)
    pa_ref[:, 0:FOX_WIDTH] = (pa[:, 0:FOX_WIDTH] * FOX_Q_SCALE).astype(BF16)
    pa_ref[:, FOX_WIDTH:] = pa[:, FOX_WIDTH:].astype(BF16)
    pg_ref[...] = jnp.dot(h, wgate_ref[...], preferred_element_type=F32)


def _proj_in(x, g, w_pm, w_pa, w_gate, *, tm):
    t, d = x.shape
    assert t % tm == 0 and w_pm.shape == (d, PM_WIDTH) and w_pa.shape == (d, PA_WIDTH)
    resident = pl.Buffered(1)
    blocks = (_nbytes((tm, d), F32) + _nbytes((d, LANES), BF16) + _nbytes((tm, PM_WIDTH), F32)
              + _nbytes((tm, PA_WIDTH), BF16) + _nbytes((tm, LANES), F32))
    scratch = _nbytes((d, PM_WIDTH + PA_WIDTH), BF16)
    temps = _nbytes((tm, d), BF16) + 2 * _nbytes((tm, PA_WIDTH), F32)
    return pl.pallas_call(
        _proj_in_kernel,
        grid=(t // tm,),
        in_specs=[
            pl.BlockSpec((tm, d), lambda i: (i, 0)),
            pl.BlockSpec((1, d), lambda i: (0, 0)),
            pl.BlockSpec((d, PM_WIDTH), lambda i: (0, 0), pipeline_mode=resident),
            pl.BlockSpec((d, PA_WIDTH), lambda i: (0, 0), pipeline_mode=resident),
            pl.BlockSpec((d, LANES), lambda i: (0, 0)),
        ],
        out_specs=[
            pl.BlockSpec((tm, PM_WIDTH), lambda i: (i, 0)),
            pl.BlockSpec((tm, PA_WIDTH), lambda i: (i, 0)),
            pl.BlockSpec((tm, LANES), lambda i: (i, 0)),
        ],
        out_shape=[
            jax.ShapeDtypeStruct((t, PM_WIDTH), F32),
            jax.ShapeDtypeStruct((t, PA_WIDTH), BF16),
            jax.ShapeDtypeStruct((t, LANES), F32),
        ],
        compiler_params=pltpu.CompilerParams(
            dimension_semantics=("parallel",),
            vmem_limit_bytes=_vmem_limit(blocks, scratch, temps)),
        name="proj_in",
    )(x, g.reshape(1, d), w_pm, w_pa, w_gate)


def _log_sigmoid(z):
    return -(jnp.maximum(-z, 0.0) + jnp.log1p(jnp.exp(-jnp.abs(z))))


def _lane_scan(v, shifts, lane_pos):
    for sh in shifts:
        v = v + jnp.where(lane_pos >= sh, pltpu.roll(v, sh, axis=1), 0.0)
    return v


def _gates_kernel(pg_ref, bias_ref, row_ref, col_ref):
    s = pg_ref.shape[0]
    z = pg_ref[...].T[0:N_GATES, :] + bias_ref[...]
    lane = lax.broadcasted_iota(jnp.int32, (1, s), 1)
    mi = z[0:MLSTM_HEADS]
    lf_m = _log_sigmoid(z[MLSTM_HEADS:2 * MLSTM_HEADS])
    lf_a = _log_sigmoid(z[2 * MLSTM_HEADS:N_GATES])
    chunk_shifts = [1 << k for k in range(MLSTM_CHUNK.bit_length() - 1)]
    seq_shifts = [1 << k for k in range((s - 1).bit_length())]
    bcum = _lane_scan(lf_m, chunk_shifts, lane % MLSTM_CHUNK)
    cfox = _lane_scan(lf_a, seq_shifts, lane)
    rows = jnp.concatenate([mi, bcum, cfox], axis=0)
    row_ref[0] = rows
    padded = jnp.concatenate([rows, jnp.zeros((LANES - N_GATES, s), F32)], axis=0)
    col_ref[...] = padded.T


def _gates(pg, bias, *, batch, seq):
    t = pg.shape[0]
    assert t == batch * seq
    blocks = 2 * _nbytes((seq, LANES), F32) + _nbytes((N_GATES, seq), F32)
    temps = 6 * _nbytes((seq, LANES), F32)
    return pl.pallas_call(
        _gates_kernel,
        grid=(batch,),
        in_specs=[
            pl.BlockSpec((seq, LANES), lambda b: (b, 0)),
            pl.BlockSpec((N_GATES, 1), lambda b: (0, 0)),
        ],
        out_specs=[
            pl.BlockSpec((1, N_GATES, seq), lambda b: (b, 0, 0)),
            pl.BlockSpec((seq, LANES), lambda b: (b, 0)),
        ],
        out_shape=[
            jax.ShapeDtypeStruct((batch, N_GATES, seq), F32),
            jax.ShapeDtypeStruct((t, LANES), F32),
        ],
        compiler_params=pltpu.CompilerParams(
            dimension_semantics=("parallel",),
            vmem_limit_bytes=_vmem_limit(blocks, 0, temps)),
        name="gates",
    )(pg, bias)


def _mlstm_kernel(qk_ref, v_ref, o_ref, gcol_ref, grow_ref, convw_ref, hg_ref, out_ref,
                  ubuf, qkc, cn_ref, m_ref, *, tb):
    step = pl.program_id(1)
    L = MLSTM_CHUNK
    halo = SUBLANES

    @pl.when(step == 0)
    def _():
        ubuf[0:halo, :] = jnp.zeros((halo, 2 * MLSTM_WIDTH), F32)
        cn_ref[...] = jnp.zeros(cn_ref.shape, F32)
        m_ref[...] = jnp.zeros(m_ref.shape, F32)

    @pl.when(step > 0)
    def _():
        ubuf[0:halo, :] = ubuf[tb:tb + halo, :]

    ubuf[halo:halo + tb, :] = qk_ref[...]
    w = convw_ref[...]
    y = ubuf[halo:halo + tb, :] * w[0:1, :]
    for j in range(1, CONV_WIDTH):
        y = y + ubuf[halo - j:halo - j + tb, :] * w[j:j + 1, :]
    qkc[...] = y * jax.nn.sigmoid(y)

    tri = (lax.broadcasted_iota(jnp.int32, (L, L), 0) >= lax.broadcasted_iota(jnp.int32, (L, L), 1))
    ones_col = (lax.broadcasted_iota(jnp.int32, (L, HEAD_DIM), 1) == 0).astype(F32)
    q_scale = HEAD_DIM ** -0.5

    for c in range(tb // L):
        rows = slice(c * L, (c + 1) * L)
        for hd in range(MLSTM_HEADS):
            hs = slice(hd * HEAD_DIM, (hd + 1) * HEAD_DIM)
            ks = slice(MLSTM_WIDTH + hd * HEAD_DIM, MLSTM_WIDTH + (hd + 1) * HEAD_DIM)
            q = (qkc[rows, hs] * q_scale).astype(BF16)
            k = qkc[rows, ks]
            k_bf = k.astype(BF16)
            vaug = jnp.concatenate([v_ref[rows, hs], ones_col], axis=1)
            i_col = gcol_ref[rows, hd:hd + 1]
            b_col = gcol_ref[rows, MLSTM_HEADS + hd:MLSTM_HEADS + hd + 1]
            i_row = grow_ref[0, hd:hd + 1, rows]
            b_row = grow_ref[0, MLSTM_HEADS + hd:MLSTM_HEADS + hd + 1, rows]
            m_prev = m_ref[hd][0:1, 0:1]
            cn = cn_ref[hd]

            log_d = jnp.where(tri, b_col - b_row + i_row, -jnp.inf)
            inter = b_col + m_prev
            m_t = jnp.maximum(inter, jnp.max(log_d, axis=-1, keepdims=True))
            s = lax.dot_general(q, k_bf, (((1,), (1,)), ((), ())), preferred_element_type=F32)
            scores = (s * jnp.exp(log_d - m_t)).astype(BF16)
            inter_w = jnp.exp(inter - m_t)
            numden = (jnp.dot(scores, vaug.astype(BF16), preferred_element_type=F32)
                      + inter_w * jnp.dot(q, cn.astype(BF16), preferred_element_type=F32))
            num = numden[:, 0:HEAD_DIM]
            den = numden[:, HEAD_DIM:HEAD_DIM + 1]
            h = num / jnp.maximum(jnp.abs(den), jnp.exp(-m_t))

            b_last = b_row[:, L - 1:L]
            log_w = b_last - b_col + i_col
            m_new = jnp.maximum(b_last + m_prev, jnp.max(log_w, axis=0, keepdims=True))
            wv = (jnp.exp(log_w - m_new) * vaug).astype(BF16)
            decay = jnp.exp(b_last + m_prev - m_new)
            cn_ref[hd] = decay * cn + jnp.dot(k.T.astype(BF16), wv, preferred_element_type=F32)
            m_ref[hd] = jnp.broadcast_to(m_new, (SUBLANES, LANES))

            hn = h * _rms_scale(h)
            out_ref[rows, hs] = (hn * hg_ref[:, hs] * jax.nn.sigmoid(o_ref[rows, hs])).astype(BF16)


def _mlstm(pm, gcol, grow, conv_w, head_g, *, batch, seq, tb):
    t = pm.shape[0]
    assert seq % tb == 0 and tb % MLSTM_CHUNK == 0
    nb = seq // tb
    w2 = 2 * MLSTM_WIDTH
    blocks = (_nbytes((tb, w2), F32) + 2 * _nbytes((tb, MLSTM_WIDTH), F32) + _nbytes((tb, LANES), F32)
              + _nbytes((N_GATES, tb), F32) + _nbytes((tb, MLSTM_WIDTH), BF16))
    scratch = (_nbytes((tb + 2 * SUBLANES, w2), F32) + _nbytes((tb, w2), F32)
               + _nbytes((MLSTM_HEADS, HEAD_DIM, 2 * HEAD_DIM), F32))
    temps = 2 * _nbytes((tb, w2), F32)
    return pl.pallas_call(
        functools.partial(_mlstm_kernel, tb=tb),
        grid=(batch, nb),
        in_specs=[
            pl.BlockSpec((tb, w2), lambda b, i: (b * nb + i, 0)),
            pl.BlockSpec((tb, MLSTM_WIDTH), lambda b, i: (b * nb + i, 2)),
            pl.BlockSpec((tb, MLSTM_WIDTH), lambda b, i: (b * nb + i, 3)),
            pl.BlockSpec((tb, LANES), lambda b, i: (b * nb + i, 0)),
            pl.BlockSpec((1, N_GATES, tb), lambda b, i: (b, 0, i)),
            pl.BlockSpec((CONV_WIDTH, w2), lambda b, i: (0, 0)),
            pl.BlockSpec((1, MLSTM_WIDTH), lambda b, i: (0, 0)),
        ],
        out_specs=pl.BlockSpec((tb, MLSTM_WIDTH), lambda b, i: (b * nb + i, 0)),
        out_shape=jax.ShapeDtypeStruct((t, MLSTM_WIDTH), BF16),
        scratch_shapes=[
            pltpu.VMEM((tb + 2 * SUBLANES, w2), F32),
            pltpu.VMEM((tb, w2), F32),
            pltpu.VMEM((MLSTM_HEADS, HEAD_DIM, 2 * HEAD_DIM), F32),
            pltpu.VMEM((MLSTM_HEADS, SUBLANES, LANES), F32),
        ],
        compiler_params=pltpu.CompilerParams(
            dimension_semantics=("parallel", "arbitrary"),
            vmem_limit_bytes=_vmem_limit(blocks, scratch, temps)),
        name="mlstm",
    )(pm, pm, pm, gcol, grow, conv_w, head_g.reshape(1, MLSTM_WIDTH))


def _pool_kernel(u_ref, w_ref, scale_ref, out_ref, ext, *, ts):
    step = pl.program_id(1)
    halo = 2 * SUBLANES
    assert halo >= max(POOL_WINDOWS)

    @pl.when(step == 0)
    def _():
        ext[0:halo, :] = jnp.zeros((halo, POOL_WIDTH), F32)

    @pl.when(step > 0)
    def _():
        ext[0:halo, :] = ext[ts:ts + halo, :]

    ext[halo:halo + ts, :] = u_ref[...]
    pos = (step * ts + lax.broadcasted_iota(jnp.int32, (ts, 1), 0) + 1).astype(F32)
    for g, win in enumerate(POOL_WINDOWS):
        cols = slice(g * POOL_GROUP_DIM, (g + 1) * POOL_GROUP_DIM)
        acc = ext[halo:halo + ts, cols]
        for j in range(1, win):
            acc = acc + ext[halo - j:halo - j + ts, cols]
        mean = acc / jnp.minimum(pos, float(win))
        diff = (mean - ext[halo:halo + ts, cols]).astype(BF16)
        y = jnp.dot(diff, w_ref[g], preferred_element_type=F32)
        out_ref[:, cols] = (y * scale_ref[:, cols]).astype(BF16)


def _pool(pm, pool_w, pool_scale, *, batch, seq, ts):
    t = pm.shape[0]
    assert seq % ts == 0
    nb = seq // ts
    col_block = (4 * MLSTM_WIDTH) // POOL_WIDTH
    blocks = (_nbytes((ts, POOL_WIDTH), F32) + _nbytes(pool_w.shape, BF16)
              + _nbytes((ts, POOL_WIDTH), BF16))
    scratch = _nbytes((ts + 2 * SUBLANES, POOL_WIDTH), F32)
    temps = 4 * _nbytes((ts, POOL_GROUP_DIM), F32)
    return pl.pallas_call(
        functools.partial(_pool_kernel, ts=ts),
        grid=(batch, nb),
        in_specs=[
            pl.BlockSpec((ts, POOL_WIDTH), lambda b, i: (b * nb + i, col_block)),
            pl.BlockSpec(pool_w.shape, lambda b, i: (0, 0, 0)),
            pl.BlockSpec((1, POOL_WIDTH), lambda b, i: (0, 0)),
        ],
        out_specs=pl.BlockSpec((ts, POOL_WIDTH), lambda b, i: (b * nb + i, 0)),
        out_shape=jax.ShapeDtypeStruct((t, POOL_WIDTH), BF16),
        scratch_shapes=[pltpu.VMEM((ts + 2 * SUBLANES, POOL_WIDTH), F32)],
        compiler_params=pltpu.CompilerParams(
            dimension_semantics=("parallel", "arbitrary"),
            vmem_limit_bytes=_vmem_limit(blocks, scratch, temps)),
        name="pool",
    )(pm, pool_w, pool_scale.reshape(1, POOL_WIDTH))


def _split3_bf16(c):
    hi = c.astype(BF16).astype(F32)
    r = c - hi
    mid = r.astype(BF16).astype(F32)
    lo = (r - mid).astype(BF16).astype(F32)
    return hi, mid, lo


def _bias_rows(c_row, n, *, query_side):
    hi, mid, lo = _split3_bf16(c_row)
    ridx = lax.broadcasted_iota(jnp.int32, (HEAD_DIM, n), 0)
    sign = 1.0 if query_side else -1.0
    first = 0 if query_side else 3
    ones_at = 3 if query_side else 0
    rows = jnp.where(ridx == first, sign * hi, 0.0)
    rows = jnp.where(ridx == first + 1, sign * mid, rows)
    rows = jnp.where(ridx == first + 2, sign * lo, rows)
    return jnp.where((ridx >= ones_at) & (ridx < ones_at + 3), 1.0, rows)


def _fox_kernel(q_ref, k_ref, v_ref, c_ref, out_ref, kt_ref, va_ref, qa_ref, m_ref, acc_ref,
                *, tq, nq, n_parts, n_diag_parts):
    qi = pl.program_id(2)

    @pl.when(qi == 0)
    def _():
        ones_col = (lax.broadcasted_iota(jnp.int32, (tq, HEAD_DIM), 1) == 0).astype(BF16)
        for jb in range(nq):
            rows = slice(jb * tq, (jb + 1) * tq)
            kt = k_ref[rows, :].astype(F32).T
            ext = _bias_rows(c_ref[0, jb:jb + 1, :] * LOG2E, tq, query_side=False)
            kt_ref[jb] = jnp.concatenate([kt, ext], axis=0).astype(BF16)
            va_ref[rows, 0:HEAD_DIM] = v_ref[rows, :]
            va_ref[rows, HEAD_DIM:] = ones_col

    c2_q = c_ref[0, pl.ds(qi, 1), :] * LOG2E
    qa_ref[:, 0:HEAD_DIM] = q_ref[...]
    qa_ref[:, HEAD_DIM:] = _bias_rows(c2_q, tq, query_side=True).T.astype(BF16)
    m_ref[...] = jnp.full(m_ref.shape, -jnp.inf, F32)
    acc_ref[...] = jnp.zeros(acc_ref.shape, F32)

    def logits(r0, nr, kv_blk, nc, masked):
        s = jnp.dot(qa_ref[r0:r0 + nr, :], kt_ref[kv_blk][:, 0:nc], preferred_element_type=F32)
        if masked:
            keep = (lax.broadcasted_iota(jnp.int32, (nr, nc), 1)
                    <= lax.broadcasted_iota(jnp.int32, (nr, nc), 0) + r0)
            s = jnp.where(keep, s, -jnp.inf)
        return s

    def accumulate(s, r0, nr, kv_blk, nc):
        m_prev = m_ref[r0:r0 + nr, :]
        m_new = jnp.maximum(m_prev, jnp.max(s, axis=-1, keepdims=True))
        alpha = jnp.exp2(m_prev - m_new)
        p = jnp.exp2(s - m_new).astype(BF16)
        v_rows = pl.ds(pl.multiple_of(kv_blk * tq, tq), nc)
        acc_ref[r0:r0 + nr, :] = (alpha * acc_ref[r0:r0 + nr, :]
                                  + jnp.dot(p, va_ref[v_rows, :], preferred_element_type=F32))
        m_ref[r0:r0 + nr, :] = m_new

    def update(parts, kv_blk, masked):
        ss = [logits(r0, nr, kv_blk, nc, masked) for r0, nr, nc in parts]
        for s, (r0, nr, nc) in zip(ss, parts):
            accumulate(s, r0, nr, kv_blk, nc)

    pr = tq // n_parts

    def body(j, carry):
        update([(i * pr, pr, tq) for i in range(n_parts)], j, False)
        return carry

    lax.fori_loop(0, qi, body, 0)
    dr = tq // n_diag_parts
    update([(i * dr, dr, (i + 1) * dr) for i in range(n_diag_parts)], qi, True)
    out_ref[...] = (acc_ref[:, 0:HEAD_DIM] / acc_ref[:, HEAD_DIM:HEAD_DIM + 1]).astype(BF16)


def _fox(pa, grow, *, batch, seq, tq, n_parts, n_diag_parts):
    t = pa.shape[0]
    assert seq % tq == 0
    nq = seq // tq
    assert tq % n_parts == 0 and (tq // n_parts) % LANES == 0
    assert tq % n_diag_parts == 0 and (tq // n_diag_parts) % LANES == 0
    c = grow.reshape(batch * N_GATES, nq, tq)
    gate0 = 2 * MLSTM_HEADS
    blocks = (2 * _nbytes((tq, HEAD_DIM), BF16) + 2 * _nbytes((seq, HEAD_DIM), BF16)
              + _nbytes((max(nq, SUBLANES), tq), F32))
    scratch = (2 * _nbytes((seq, 2 * HEAD_DIM), BF16) + _nbytes((tq, 2 * HEAD_DIM), BF16)
               + _nbytes((tq, LANES), F32) + _nbytes((tq, 2 * HEAD_DIM), F32))
    temps = 3 * _nbytes((tq, tq), F32)
    return pl.pallas_call(
        functools.partial(_fox_kernel, tq=tq, nq=nq, n_parts=n_parts, n_diag_parts=n_diag_parts),
        grid=(batch, FOX_HEADS, nq),
        in_specs=[
            pl.BlockSpec((tq, HEAD_DIM), lambda b, h, i: (b * nq + i, h)),
            pl.BlockSpec((seq, HEAD_DIM), lambda b, h, i: (b, FOX_HEADS + h)),
            pl.BlockSpec((seq, HEAD_DIM), lambda b, h, i: (b, 2 * FOX_HEADS + h)),
            pl.BlockSpec((1, nq, tq), lambda b, h, i: (b * N_GATES + gate0 + h, 0, 0)),
        ],
        out_specs=pl.BlockSpec((tq, HEAD_DIM), lambda b, h, i: (b * nq + i, h)),
        out_shape=jax.ShapeDtypeStruct((t, FOX_WIDTH), BF16),
        scratch_shapes=[
            pltpu.VMEM((nq, 2 * HEAD_DIM, tq), BF16),
            pltpu.VMEM((seq, 2 * HEAD_DIM), BF16),
            pltpu.VMEM((tq, 2 * HEAD_DIM), BF16),
            pltpu.VMEM((tq, 1), F32),
            pltpu.VMEM((tq, 2 * HEAD_DIM), F32),
        ],
        compiler_params=pltpu.CompilerParams(
            dimension_semantics=("parallel", "parallel", "arbitrary"),
            vmem_limit_bytes=_vmem_limit(blocks, scratch, temps)),
        name="fox",
    )(pa, pa, pa, c)


def _proj_out_kernel(hm_ref, hp_ref, ha_ref, w_ref, x_ref, g_ref, o_ref):
    tm = x_ref.shape[0]
    y = jnp.dot(hm_ref[...], w_ref[0:MLSTM_WIDTH, :], preferred_element_type=F32)
    y = y + jnp.dot(hp_ref[...], w_ref[MLSTM_WIDTH:MLSTM_WIDTH + POOL_WIDTH, :],
                    preferred_element_type=F32)
    y = y + jnp.dot(ha_ref[...], w_ref[MLSTM_WIDTH + POOL_WIDTH:, :], preferred_element_type=F32)
    o_ref[...] = y
    _postnorm_residual(x_ref, o_ref, g_ref, o_ref, 1.0)


def _proj_out(hm, hp, ha, w_out, x, g, *, tm):
    t, d = x.shape
    assert t % tm == 0
    blocks = (_nbytes((tm, d), BF16) + _nbytes((d, d), BF16) + 2 * _nbytes((tm, d), F32))
    temps = 2 * _nbytes((tm, d), F32)
    return pl.pallas_call(
        _proj_out_kernel,
        grid=(t // tm,),
        in_specs=[
            pl.BlockSpec((tm, MLSTM_WIDTH), lambda i: (i, 0)),
            pl.BlockSpec((tm, POOL_WIDTH), lambda i: (i, 0)),
            pl.BlockSpec((tm, FOX_WIDTH), lambda i: (i, 0)),
            pl.BlockSpec((d, d), lambda i: (0, 0)),
            pl.BlockSpec((tm, d), lambda i: (i, 0)),
            pl.BlockSpec((1, d), lambda i: (0, 0)),
        ],
        out_specs=pl.BlockSpec((tm, d), lambda i: (i, 0)),
        out_shape=jax.ShapeDtypeStruct((t, d), F32),
        compiler_params=pltpu.CompilerParams(
            dimension_semantics=("parallel",),
            vmem_limit_bytes=_vmem_limit(blocks, 0, temps)),
        name="proj_out",
    )(hm, hp, ha, w_out, x, g.reshape(1, d))


def _mixer(x, pre_g, post_g, w_in_parts, conv_w, b_i, b_f, head_g, pool_w, pool_scale, fox_b_f,
           w_out, *, batch, seq):
    w_pm, w_pa, w_gate = w_in_parts
    bias = jnp.concatenate([b_i, b_f, fox_b_f]).reshape(N_GATES, 1)

    pm, pa, pg = _proj_in(x, pre_g, w_pm, w_pa, w_gate, tm=512)
    grow, gcol = _gates(pg, bias, batch=batch, seq=seq)
    hm = _mlstm(pm, gcol, grow, conv_w, head_g, batch=batch, seq=seq, tb=min(512, seq))
    hp = _pool(pm, pool_w.astype(BF16), pool_scale, batch=batch, seq=seq, ts=min(512, seq))
    ha = _fox(pa, grow, batch=batch, seq=seq, tq=min(2048, seq), n_parts=4, n_diag_parts=8)
    return _proj_out(hm, hp, ha, w_out, x, post_g, tm=512)


def kernel(x, ffn1_pre_g, ffn1_post_g, ffn1_w_gate, ffn1_w_up, ffn1_w_down, mix_pre_g, mix_post_g,
           w_in, mlstm_conv, mlstm_b_i, mlstm_b_f, mlstm_head_g, pool_w, pool_scale, fox_b_f, w_out,
           ffn2_pre_g, ffn2_post_g, ffn2_w_gate, ffn2_w_up, ffn2_w_down):
    batch, seq, d = x.shape
    depth = w_in.shape[0]
    xt = x.reshape(batch * seq, d)
    for l in range(depth):
        xt = _ffn(xt, ffn1_pre_g[l], ffn1_post_g[l], _layer_bf16(ffn1_w_gate, l),
                  _layer_bf16(ffn1_w_up, l), _layer_bf16(ffn1_w_down, l), tm=1024, tf=512)
        xt = _mixer(xt, mix_pre_g[l], mix_post_g[l], _split_w_in(w_in, l, tr=256), mlstm_conv[l],
                    mlstm_b_i[l],
                    mlstm_b_f[l], mlstm_head_g[l], pool_w[l], pool_scale[l], fox_b_f[l],
                    _layer_bf16(w_out, l), batch=batch, seq=seq)
        xt = _ffn(xt, ffn2_pre_g[l], ffn2_post_g[l], _layer_bf16(ffn2_w_gate, l),
                  _layer_bf16(ffn2_w_up, l), _layer_bf16(ffn2_w_down, l), tm=1024, tf=512)
    return xt.reshape(batch, seq, d)
```

```python
import functools

import jax
import jax.numpy as jnp
from jax import lax
from jax.experimental import pallas as pl
from jax.experimental.pallas import tpu as pltpu

F32 = jnp.float32
BF16 = jnp.bfloat16

D_MODEL = 2048
HEAD_DIM = 128
MLSTM_WIDTH = D_MODEL // 4
MLSTM_HEADS = MLSTM_WIDTH // HEAD_DIM
POOL_WIDTH = D_MODEL // 4
POOL_WINDOWS = (2, 4, 8, 16)
POOL_GROUP_DIM = POOL_WIDTH // len(POOL_WINDOWS)
FOX_WIDTH = D_MODEL - MLSTM_WIDTH - POOL_WIDTH
FOX_HEADS = FOX_WIDTH // HEAD_DIM
CONV_WIDTH = 4
MLSTM_CHUNK = 128
RMS_EPS = 1e-6
FFN_RESIDUAL_WEIGHT = 0.5

OFF_MI = 4 * MLSTM_WIDTH
OFF_MF = OFF_MI + MLSTM_HEADS
OFF_POOL = OFF_MF + MLSTM_HEADS
OFF_AQ = OFF_POOL + POOL_WIDTH
OFF_AF = OFF_AQ + 3 * FOX_WIDTH
N_IN = OFF_AF + FOX_HEADS

PM_WIDTH = 4 * MLSTM_WIDTH + POOL_WIDTH
PA_WIDTH = 3 * FOX_WIDTH
N_GATES = 2 * MLSTM_HEADS + FOX_HEADS
LOG2E = 1.4426950408889634
FOX_Q_SCALE = LOG2E * HEAD_DIM ** -0.5
LANES = 128
SUBLANES = 8
VMEM_LIMIT_CAP = 58 * 1024 * 1024


def _vmem_limit(block_bytes, scratch_bytes, temp_bytes):
    return int(min(2 * block_bytes + scratch_bytes + temp_bytes + (4 << 20), VMEM_LIMIT_CAP))


def _nbytes(shape, dtype):
    n = 1
    for s in shape:
        n *= s
    return n * jnp.dtype(dtype).itemsize


def _rms_scale(x):
    return lax.rsqrt(jnp.mean(x * x, axis=-1, keepdims=True) + RMS_EPS)


CAST_BLOCK_BYTES = 6 << 20


def _cast_kernel(w_ref, o_ref):
    o_ref[...] = w_ref[...].astype(BF16)


def _layer_bf16(w, layer):
    _, r, c = w.shape
    tr = r
    while _nbytes((tr, c), F32) > CAST_BLOCK_BYTES and tr % 2 == 0 and (tr // 2) % 16 == 0:
        tr //= 2
    assert r % tr == 0
    blocks = _nbytes((tr, c), F32) + _nbytes((tr, c), BF16)
    return pl.pallas_call(
        _cast_kernel,
        grid=(r // tr,),
        in_specs=[pl.BlockSpec((None, tr, c), lambda i: (layer, i, 0))],
        out_specs=pl.BlockSpec((tr, c), lambda i: (i, 0)),
        out_shape=jax.ShapeDtypeStruct((r, c), BF16),
        compiler_params=pltpu.CompilerParams(
            dimension_semantics=("parallel",),
            vmem_limit_bytes=_vmem_limit(blocks, 0, 0)),
        name="cast",
    )(w)


def _split_w_in_kernel(wt_ref, pm_ref, pa_ref, gate_ref):
    tr = wt_ref.shape[1]
    pm_ref[:, 0:OFF_MI] = wt_ref[0:OFF_MI, :].T.astype(BF16)
    pm_ref[:, OFF_MI:PM_WIDTH] = wt_ref[OFF_POOL:OFF_AQ, :].T.astype(BF16)
    pa_ref[...] = wt_ref[OFF_AQ:OFF_AF, :].T.astype(BF16)
    gates = jnp.concatenate([wt_ref[OFF_MI:OFF_POOL, :], wt_ref[OFF_AF:N_IN, :],
                             jnp.zeros((LANES - N_GATES, tr), F32)], axis=0)
    gate_ref[...] = gates.T.astype(BF16)


def _split_w_in(w_in, layer, *, tr):
    _, r, c = w_in.shape
    assert r % tr == 0 and c == N_IN and tr % LANES == 0
    assert all(off % SUBLANES == 0 for off in (OFF_MI, OFF_POOL, OFF_AQ, OFF_AF))
    blocks = _nbytes((c, tr), F32) + _nbytes((tr, PM_WIDTH + PA_WIDTH + LANES), BF16)
    return pl.pallas_call(
        _split_w_in_kernel,
        grid=(r // tr,),
        in_specs=[pl.BlockSpec((None, c, tr), lambda i: (layer, 0, i))],
        out_specs=[
            pl.BlockSpec((tr, PM_WIDTH), lambda i: (i, 0)),
            pl.BlockSpec((tr, PA_WIDTH), lambda i: (i, 0)),
            pl.BlockSpec((tr, LANES), lambda i: (i, 0)),
        ],
        out_shape=[
            jax.ShapeDtypeStruct((r, PM_WIDTH), BF16),
            jax.ShapeDtypeStruct((r, PA_WIDTH), BF16),
            jax.ShapeDtypeStruct((r, LANES), BF16),
        ],
        compiler_params=pltpu.CompilerParams(
            dimension_semantics=("parallel",),
            vmem_limit_bytes=_vmem_limit(blocks, 0, 2 * _nbytes((tr, c), F32))),
        name="split_w_in",
    )(jnp.swapaxes(w_in, 1, 2))


NORM_ROWS = 32
NORM_COLS = 512


def _row_rms_scale(ref, rows):
    d = ref.shape[1]
    ss = None
    for c0 in range(0, d, NORM_COLS):
        v = ref[rows, c0:c0 + NORM_COLS]
        ss = v * v if ss is None else ss + v * v
    return lax.rsqrt(jnp.sum(ss, axis=-1, keepdims=True) * (1.0 / d) + RMS_EPS)


def _normalise_rows(n_rows, src_ref, apply_fn):
    groups = 4
    trip_rows = groups * NORM_ROWS
    n_trips = n_rows // trip_rows

    def rows_of(r):
        base = pl.multiple_of(r * trip_rows, trip_rows)
        return [pl.ds(base + k * NORM_ROWS, NORM_ROWS) for k in range(groups)]

    def scales_of(r):
        return tuple(_row_rms_scale(src_ref, rows) for rows in rows_of(r))

    def body(r, scales):
        nxt = scales_of(jnp.minimum(r + 1, n_trips - 1))
        for rows, scale in zip(rows_of(r), scales):
            apply_fn(rows, scale)
        return nxt

    lax.fori_loop(0, n_trips, body, scales_of(0))


def _prenorm(x_ref, g_ref, h_ref):
    def apply(rows, scale):
        for c0 in range(0, x_ref.shape[1], NORM_COLS):
            cols = slice(c0, c0 + NORM_COLS)
            h_ref[rows, cols] = (x_ref[rows, cols] * scale * g_ref[:, cols]).astype(BF16)
    _normalise_rows(x_ref.shape[0], x_ref, apply)


def _postnorm_residual(x_ref, y_ref, g_ref, o_ref, weight):
    def apply(rows, scale):
        scale = scale * weight
        for c0 in range(0, x_ref.shape[1], NORM_COLS):
            cols = slice(c0, c0 + NORM_COLS)
            o_ref[rows, cols] = x_ref[rows, cols] + y_ref[rows, cols] * scale * g_ref[:, cols]
    _normalise_rows(x_ref.shape[0], y_ref, apply)


def _ffn_kernel(x_ref, pre_g_ref, post_g_ref, wg_ref, wu_ref, wd_ref, o_ref, h_ref, *, n_f):
    j = pl.program_id(1)
    tm = x_ref.shape[0]

    @pl.when(j == 0)
    def _():
        _prenorm(x_ref, pre_g_ref, h_ref)
        o_ref[...] = jnp.zeros(o_ref.shape, F32)

    h = h_ref[...]
    g = jnp.dot(h, wg_ref[...], preferred_element_type=F32)
    u = jnp.dot(h, wu_ref[...], preferred_element_type=F32)
    a = (g * jax.nn.sigmoid(g) * u).astype(BF16)
    o_ref[...] += jnp.dot(a, wd_ref[...], preferred_element_type=F32)

    @pl.when(j == n_f - 1)
    def _():
        _postnorm_residual(x_ref, o_ref, post_g_ref, o_ref, FFN_RESIDUAL_WEIGHT)


def _ffn(x, pre_g, post_g, wg, wu, wd, *, tm, tf):
    t, d = x.shape
    f = wg.shape[1]
    assert t % tm == 0 and f % tf == 0 and FFN_RESIDUAL_WEIGHT == 0.5
    blocks = (_nbytes((tm, d), F32) * 2 + 2 * _nbytes((d, tf), BF16) + _nbytes((tf, d), BF16))
    scratch = _nbytes((tm, d), BF16)
    temps = 3 * _nbytes((tm, tf), F32)
    return pl.pallas_call(
        functools.partial(_ffn_kernel, n_f=f // tf),
        grid=(t // tm, f // tf),
        in_specs=[
            pl.BlockSpec((tm, d), lambda i, j: (i, 0)),
            pl.BlockSpec((1, d), lambda i, j: (0, 0)),
            pl.BlockSpec((1, d), lambda i, j: (0, 0)),
            pl.BlockSpec((d, tf), lambda i, j: (0, j)),
            pl.BlockSpec((d, tf), lambda i, j: (0, j)),
            pl.BlockSpec((tf, d), lambda i, j: (j, 0)),
        ],
        out_specs=pl.BlockSpec((tm, d), lambda i, j: (i, 0)),
        out_shape=jax.ShapeDtypeStruct((t, d), F32),
        scratch_shapes=[pltpu.VMEM((tm, d), BF16)],
        compiler_params=pltpu.CompilerParams(
            dimension_semantics=("parallel", "arbitrary"),
            vmem_limit_bytes=_vmem_limit(blocks, scratch, temps)),
        name="ffn",
    )(x, pre_g.reshape(1, d), post_g.reshape(1, d), wg, wu, wd)


def _proj_in_kernel(x_ref, g_ref, wpm_ref, wpa_ref, wgate_ref, pm_ref, pa_ref, pg_ref):
    x = x_ref[...]
    h = (x * _rms_scale(x) * g_ref[...]).astype(BF16)
    pm_ref[...] = jnp.dot(h, wpm_ref[...], preferred_element_type=F32)
    pa = jnp.dot(h, wpa_ref[...], preferred_element_type=F32)
    pa_ref[:, 0:FOX_WIDTH] = (pa[:, 0:FOX_WIDTH] * FOX_Q_SCALE).astype(BF16)
    pa_ref[:, FOX_WIDTH:] = pa[:, FOX_WIDTH:].astype(BF16)
    pg_ref[...] = jnp.dot(h, wgate_ref[...], preferred_element_type=F32)


def _proj_in(x, g, w_pm, w_pa, w_gate, *, tm):
    t, d = x.shape
    assert t % tm == 0 and w_pm.shape == (d, PM_WIDTH) and w_pa.shape == (d, PA_WIDTH)
    resident = pl.Buffered(1)
    blocks = (_nbytes((tm, d), F32) + _nbytes((d, LANES), BF16) + _nbytes((tm, PM_WIDTH), F32)
              + _nbytes((tm, PA_WIDTH), BF16) + _nbytes((tm, LANES), F32))
    scratch = _nbytes((d, PM_WIDTH + PA_WIDTH), BF16)
    temps = _nbytes((tm, d), BF16) + 2 * _nbytes((tm, PA_WIDTH), F32)
    return pl.pallas_call(
        _proj_in_kernel,
        grid=(t // tm,),
        in_specs=[
            pl.BlockSpec((tm, d), lambda i: (i, 0)),
            pl.BlockSpec((1, d), lambda i: (0, 0)),
            pl.BlockSpec((d, PM_WIDTH), lambda i: (0, 0), pipeline_mode=resident),
            pl.BlockSpec((d, PA_WIDTH), lambda i: (0, 0), pipeline_mode=resident),
            pl.BlockSpec((d, LANES), lambda i: (0, 0)),
        ],
        out_specs=[
            pl.BlockSpec((tm, PM_WIDTH), lambda i: (i, 0)),
            pl.BlockSpec((tm, PA_WIDTH), lambda i: (i, 0)),
            pl.BlockSpec((tm, LANES), lambda i: (i, 0)),
        ],
        out_shape=[
            jax.ShapeDtypeStruct((t, PM_WIDTH), F32),
            jax.ShapeDtypeStruct((t, PA_WIDTH), BF16),
            jax.ShapeDtypeStruct((t, LANES), F32),
        ],
        compiler_params=pltpu.CompilerParams(
            dimension_semantics=("parallel",),
            vmem_limit_bytes=_vmem_limit(blocks, scratch, temps)),
        name="proj_in",
    )(x, g.reshape(1, d), w_pm, w_pa, w_gate)


def _log_sigmoid(z):
    return -(jnp.maximum(-z, 0.0) + jnp.log1p(jnp.exp(-jnp.abs(z))))


def _lane_scan(v, shifts, lane_pos):
    for sh in shifts:
        v = v + jnp.where(lane_pos >= sh, pltpu.roll(v, sh, axis=1), 0.0)
    return v


def _gates_kernel(pg_ref, bias_ref, row_ref, col_ref):
    s = pg_ref.shape[0]
    z = pg_ref[...].T[0:N_GATES, :] + bias_ref[...]
    lane = lax.broadcasted_iota(jnp.int32, (1, s), 1)
    mi = z[0:MLSTM_HEADS]
    lf_m = _log_sigmoid(z[MLSTM_HEADS:2 * MLSTM_HEADS])
    lf_a = _log_sigmoid(z[2 * MLSTM_HEADS:N_GATES])
    chunk_shifts = [1 << k for k in range(MLSTM_CHUNK.bit_length() - 1)]
    seq_shifts = [1 << k for k in range((s - 1).bit_length())]
    bcum = _lane_scan(lf_m, chunk_shifts, lane % MLSTM_CHUNK)
    cfox = _lane_scan(lf_a, seq_shifts, lane)
    rows = jnp.concatenate([mi, bcum, cfox], axis=0)
    row_ref[0] = rows
    padded = jnp.concatenate([rows, jnp.zeros((LANES - N_GATES, s), F32)], axis=0)
    col_ref[...] = padded.T


def _gates(pg, bias, *, batch, seq):
    t = pg.shape[0]
    assert t == batch * seq
    blocks = 2 * _nbytes((seq, LANES), F32) + _nbytes((N_GATES, seq), F32)
    temps = 6 * _nbytes((seq, LANES), F32)
    return pl.pallas_call(
        _gates_kernel,
        grid=(batch,),
        in_specs=[
            pl.BlockSpec((seq, LANES), lambda b: (b, 0)),
            pl.BlockSpec((N_GATES, 1), lambda b: (0, 0)),
        ],
        out_specs=[
            pl.BlockSpec((1, N_GATES, seq), lambda b: (b, 0, 0)),
            pl.BlockSpec((seq, LANES), lambda b: (b, 0)),
        ],
        out_shape=[
            jax.ShapeDtypeStruct((batch, N_GATES, seq), F32),
            jax.ShapeDtypeStruct((t, LANES), F32),
        ],
        compiler_params=pltpu.CompilerParams(
            dimension_semantics=("parallel",),
            vmem_limit_bytes=_vmem_limit(blocks, 0, temps)),
        name="gates",
    )(pg, bias)


def _mlstm_kernel(qk_ref, v_ref, o_ref, gcol_ref, grow_ref, convw_ref, hg_ref, out_ref,
                  ubuf, qkc, cn_ref, m_ref, *, tb):
    step = pl.program_id(1)
    L = MLSTM_CHUNK
    halo = SUBLANES

    @pl.when(step == 0)
    def _():
        ubuf[0:halo, :] = jnp.zeros((halo, 2 * MLSTM_WIDTH), F32)
        cn_ref[...] = jnp.zeros(cn_ref.shape, F32)
        m_ref[...] = jnp.zeros(m_ref.shape, F32)

    @pl.when(step > 0)
    def _():
        ubuf[0:halo, :] = ubuf[tb:tb + halo, :]

    ubuf[halo:halo + tb, :] = qk_ref[...]
    w = convw_ref[...]
    y = ubuf[halo:halo + tb, :] * w[0:1, :]
    for j in range(1, CONV_WIDTH):
        y = y + ubuf[halo - j:halo - j + tb, :] * w[j:j + 1, :]
    qkc[...] = y * jax.nn.sigmoid(y)

    tri = (lax.broadcasted_iota(jnp.int32, (L, L), 0) >= lax.broadcasted_iota(jnp.int32, (L, L), 1))
    ones_col = (lax.broadcasted_iota(jnp.int32, (L, HEAD_DIM), 1) == 0).astype(F32)
    q_scale = HEAD_DIM ** -0.5

    for c in range(tb // L):
        rows = slice(c * L, (c + 1) * L)
        for hd in range(MLSTM_HEADS):
            hs = slice(hd * HEAD_DIM, (hd + 1) * HEAD_DIM)
            ks = slice(MLSTM_WIDTH + hd * HEAD_DIM, MLSTM_WIDTH + (hd + 1) * HEAD_DIM)
            q = (qkc[rows, hs] * q_scale).astype(BF16)
            k = qkc[rows, ks]
            k_bf = k.astype(BF16)
            vaug = jnp.concatenate([v_ref[rows, hs], ones_col], axis=1)
            i_col = gcol_ref[rows, hd:hd + 1]
            b_col = gcol_ref[rows, MLSTM_HEADS + hd:MLSTM_HEADS + hd + 1]
            i_row = grow_ref[0, hd:hd + 1, rows]
            b_row = grow_ref[0, MLSTM_HEADS + hd:MLSTM_HEADS + hd + 1, rows]
            m_prev = m_ref[hd][0:1, 0:1]
            cn = cn_ref[hd]

            log_d = jnp.where(tri, b_col - b_row + i_row, -jnp.inf)
            inter = b_col + m_prev
            m_t = jnp.maximum(inter, jnp.max(log_d, axis=-1, keepdims=True))
            s = lax.dot_general(q, k_bf, (((1,), (1,)), ((), ())), preferred_element_type=F32)
            scores = (s * jnp.exp(log_d - m_t)).astype(BF16)
            inter_w = jnp.exp(inter - m_t)
            numden = (jnp.dot(scores, vaug.astype(BF16), preferred_element_type=F32)
                      + inter_w * jnp.dot(q, cn.astype(BF16), preferred_element_type=F32))
            num = numden[:, 0:HEAD_DIM]
            den = numden[:, HEAD_DIM:HEAD_DIM + 1]
            h = num / jnp.maximum(jnp.abs(den), jnp.exp(-m_t))

            b_last = b_row[:, L - 1:L]
            log_w = b_last - b_col + i_col
            m_new = jnp.maximum(b_last + m_prev, jnp.max(log_w, axis=0, keepdims=True))
            wv = (jnp.exp(log_w - m_new) * vaug).astype(BF16)
            decay = jnp.exp(b_last + m_prev - m_new)
            cn_ref[hd] = decay * cn + jnp.dot(k.T.astype(BF16), wv, preferred_element_type=F32)
            m_ref[hd] = jnp.broadcast_to(m_new, (SUBLANES, LANES))

            hn = h * _rms_scale(h)
            out_ref[rows, hs] = (hn * hg_ref[:, hs] * jax.nn.sigmoid(o_ref[rows, hs])).astype(BF16)


def _mlstm(pm, gcol, grow, conv_w, head_g, *, batch, seq, tb):
    t = pm.shape[0]
    assert seq % tb == 0 and tb % MLSTM_CHUNK == 0
    nb = seq // tb
    w2 = 2 * MLSTM_WIDTH
    blocks = (_nbytes((tb, w2), F32) + 2 * _nbytes((tb, MLSTM_WIDTH), F32) + _nbytes((tb, LANES), F32)
              + _nbytes((N_GATES, tb), F32) + _nbytes((tb, MLSTM_WIDTH), BF16))
    scratch = (_nbytes((tb + 2 * SUBLANES, w2), F32) + _nbytes((tb, w2), F32)
               + _nbytes((MLSTM_HEADS, HEAD_DIM, 2 * HEAD_DIM), F32))
    temps = 2 * _nbytes((tb, w2), F32)
    return pl.pallas_call(
        functools.partial(_mlstm_kernel, tb=tb),
        grid=(batch, nb),
        in_specs=[
            pl.BlockSpec((tb, w2), lambda b, i: (b * nb + i, 0)),
            pl.BlockSpec((tb, MLSTM_WIDTH), lambda b, i: (b * nb + i, 2)),
            pl.BlockSpec((tb, MLSTM_WIDTH), lambda b, i: (b * nb + i, 3)),
            pl.BlockSpec((tb, LANES), lambda b, i: (b * nb + i, 0)),
            pl.BlockSpec((1, N_GATES, tb), lambda b, i: (b, 0, i)),
            pl.BlockSpec((CONV_WIDTH, w2), lambda b, i: (0, 0)),
            pl.BlockSpec((1, MLSTM_WIDTH), lambda b, i: (0, 0)),
        ],
        out_specs=pl.BlockSpec((tb, MLSTM_WIDTH), lambda b, i: (b * nb + i, 0)),
        out_shape=jax.ShapeDtypeStruct((t, MLSTM_WIDTH), BF16),
        scratch_shapes=[
            pltpu.VMEM((tb + 2 * SUBLANES, w2), F32),
            pltpu.VMEM((tb, w2), F32),
            pltpu.VMEM((MLSTM_HEADS, HEAD_DIM, 2 * HEAD_DIM), F32),
            pltpu.VMEM((MLSTM_HEADS, SUBLANES, LANES), F32),
        ],
        compiler_params=pltpu.CompilerParams(
            dimension_semantics=("parallel", "arbitrary"),
            vmem_limit_bytes=_vmem_limit(blocks, scratch, temps)),
        name="mlstm",
    )(pm, pm, pm, gcol, grow, conv_w, head_g.reshape(1, MLSTM_WIDTH))


def _pool_kernel(u_ref, w_ref, scale_ref, out_ref, ext, *, ts):
    step = pl.program_id(1)
    halo = 2 * SUBLANES
    assert halo >= max(POOL_WINDOWS)

    @pl.when(step == 0)
    def _():
        ext[0:halo, :] = jnp.zeros((halo, POOL_WIDTH), F32)

    @pl.when(step > 0)
    def _():
        ext[0:halo, :] = ext[ts:ts + halo, :]

    ext[halo:halo + ts, :] = u_ref[...]
    pos = (step * ts + lax.broadcasted_iota(jnp.int32, (ts, 1), 0) + 1).astype(F32)
    for g, win in enumerate(POOL_WINDOWS):
        cols = slice(g * POOL_GROUP_DIM, (g + 1) * POOL_GROUP_DIM)
        acc = ext[halo:halo + ts, cols]
        for j in range(1, win):
            acc = acc + ext[halo - j:halo - j + ts, cols]
        mean = acc / jnp.minimum(pos, float(win))
        diff = (mean - ext[halo:halo + ts, cols]).astype(BF16)
        y = jnp.dot(diff, w_ref[g], preferred_element_type=F32)
        out_ref[:, cols] = (y * scale_ref[:, cols]).astype(BF16)


def _pool(pm, pool_w, pool_scale, *, batch, seq, ts):
    t = pm.shape[0]
    assert seq % ts == 0
    nb = seq // ts
    col_block = (4 * MLSTM_WIDTH) // POOL_WIDTH
    blocks = (_nbytes((ts, POOL_WIDTH), F32) + _nbytes(pool_w.shape, BF16)
              + _nbytes((ts, POOL_WIDTH), BF16))
    scratch = _nbytes((ts + 2 * SUBLANES, POOL_WIDTH), F32)
    temps = 4 * _nbytes((ts, POOL_GROUP_DIM), F32)
    return pl.pallas_call(
        functools.partial(_pool_kernel, ts=ts),
        grid=(batch, nb),
        in_specs=[
            pl.BlockSpec((ts, POOL_WIDTH), lambda b, i: (b * nb + i, col_block)),
            pl.BlockSpec(pool_w.shape, lambda b, i: (0, 0, 0)),
            pl.BlockSpec((1, POOL_WIDTH), lambda b, i: (0, 0)),
        ],
        out_specs=pl.BlockSpec((ts, POOL_WIDTH), lambda b, i: (b * nb + i, 0)),
        out_shape=jax.ShapeDtypeStruct((t, POOL_WIDTH), BF16),
        scratch_shapes=[pltpu.VMEM((ts + 2 * SUBLANES, POOL_WIDTH), F32)],
        compiler_params=pltpu.CompilerParams(
            dimension_semantics=("parallel", "arbitrary"),
            vmem_limit_bytes=_vmem_limit(blocks, scratch, temps)),
        name="pool",
    )(pm, pool_w, pool_scale.reshape(1, POOL_WIDTH))


def _split3_bf16(c):
    hi = c.astype(BF16).astype(F32)
    r = c - hi
    mid = r.astype(BF16).astype(F32)
    lo = (r - mid).astype(BF16).astype(F32)
    return hi, mid, lo


def _bias_rows(c_row, n, *, query_side):
    hi, mid, lo = _split3_bf16(c_row)
    ridx = lax.broadcasted_iota(jnp.int32, (HEAD_DIM, n), 0)
    sign = 1.0 if query_side else -1.0
    first = 0 if query_side else 3
    ones_at = 3 if query_side else 0
    rows = jnp.where(ridx == first, sign * hi, 0.0)
    rows = jnp.where(ridx == first + 1, sign * mid, rows)
    rows = jnp.where(ridx == first + 2, sign * lo, rows)
    return jnp.where((ridx >= ones_at) & (ridx < ones_at + 3), 1.0, rows)


def _fox_kernel(q_ref, k_ref, v_ref, c_ref, out_ref, kt_ref, va_ref, qa_ref, m_ref, acc_ref,
                *, tq, nq, n_parts, n_diag_parts):
    qi = pl.program_id(2)

    @pl.when(qi == 0)
    def _():
        ones_col = (lax.broadcasted_iota(jnp.int32, (tq, HEAD_DIM), 1) == 0).astype(BF16)
        for jb in range(nq):
            rows = slice(jb * tq, (jb + 1) * tq)
            kt = k_ref[rows, :].astype(F32).T
            ext = _bias_rows(c_ref[0, jb:jb + 1, :] * LOG2E, tq, query_side=False)
            kt_ref[jb] = jnp.concatenate([kt, ext], axis=0).astype(BF16)
            va_ref[rows, 0:HEAD_DIM] = v_ref[rows, :]
            va_ref[rows, HEAD_DIM:] = ones_col

    c2_q = c_ref[0, pl.ds(qi, 1), :] * LOG2E
    qa_ref[:, 0:HEAD_DIM] = q_ref[...]
    qa_ref[:, HEAD_DIM:] = _bias_rows(c2_q, tq, query_side=True).T.astype(BF16)
    m_ref[...] = jnp.full(m_ref.shape, -jnp.inf, F32)
    acc_ref[...] = jnp.zeros(acc_ref.shape, F32)

    def logits(r0, nr, kv_blk, nc, masked):
        s = jnp.dot(qa_ref[r0:r0 + nr, :], kt_ref[kv_blk][:, 0:nc], preferred_element_type=F32)
        if masked:
            keep = (lax.broadcasted_iota(jnp.int32, (nr, nc), 1)
                    <= lax.broadcasted_iota(jnp.int32, (nr, nc), 0) + r0)
            s = jnp.where(keep, s, -jnp.inf)
        return s

    def accumulate(s, r0, nr, kv_blk, nc):
        m_prev = m_ref[r0:r0 + nr, :]
        m_new = jnp.maximum(m_prev, jnp.max(s, axis=-1, keepdims=True))
        alpha = jnp.exp2(m_prev - m_new)
        p = jnp.exp2(s - m_new).astype(BF16)
        v_rows = pl.ds(pl.multiple_of(kv_blk * tq, tq), nc)
        acc_ref[r0:r0 + nr, :] = (alpha * acc_ref[r0:r0 + nr, :]
                                  + jnp.dot(p, va_ref[v_rows, :], preferred_element_type=F32))
        m_ref[r0:r0 + nr, :] = m_new

    def update(parts, kv_blk, masked):
        ss = [logits(r0, nr, kv_blk, nc, masked) for r0, nr, nc in parts]
        for s, (r0, nr, nc) in zip(ss, parts):
            accumulate(s, r0, nr, kv_blk, nc)

    pr = tq // n_parts

    def body(j, carry):
        update([(i * pr, pr, tq) for i in range(n_parts)], j, False)
        return carry

    lax.fori_loop(0, qi, body, 0)
    dr = tq // n_diag_parts
    update([(i * dr, dr, (i + 1) * dr) for i in range(n_diag_parts)], qi, True)
    out_ref[...] = (acc_ref[:, 0:HEAD_DIM] / acc_ref[:, HEAD_DIM:HEAD_DIM + 1]).astype(BF16)


def _fox(pa, grow, *, batch, seq, tq, n_parts, n_diag_parts):
    t = pa.shape[0]
    assert seq % tq == 0
    nq = seq // tq
    assert tq % n_parts == 0 and (tq // n_parts) % LANES == 0
    assert tq % n_diag_parts == 0 and (tq // n_diag_parts) % LANES == 0
    c = grow.reshape(batch * N_GATES, nq, tq)
    gate0 = 2 * MLSTM_HEADS
    blocks = (2 * _nbytes((tq, HEAD_DIM), BF16) + 2 * _nbytes((seq, HEAD_DIM), BF16)
              + _nbytes((max(nq, SUBLANES), tq), F32))
    scratch = (2 * _nbytes((seq, 2 * HEAD_DIM), BF16) + _nbytes((tq, 2 * HEAD_DIM), BF16)
               + _nbytes((tq, LANES), F32) + _nbytes((tq, 2 * HEAD_DIM), F32))
    temps = 3 * _nbytes((tq, tq), F32)
    return pl.pallas_call(
        functools.partial(_fox_kernel, tq=tq, nq=nq, n_parts=n_parts, n_diag_parts=n_diag_parts),
        grid=(batch, FOX_HEADS, nq),
        in_specs=[
            pl.BlockSpec((tq, HEAD_DIM), lambda b, h, i: (b * nq + i, h)),
            pl.BlockSpec((seq, HEAD_DIM), lambda b, h, i: (b, FOX_HEADS + h)),
            pl.BlockSpec((seq, HEAD_DIM), lambda b, h, i: (b, 2 * FOX_HEADS + h)),
            pl.BlockSpec((1, nq, tq), lambda b, h, i: (b * N_GATES + gate0 + h, 0, 0)),
        ],
        out_specs=pl.BlockSpec((tq, HEAD_DIM), lambda b, h, i: (b * nq + i, h)),
        out_shape=jax.ShapeDtypeStruct((t, FOX_WIDTH), BF16),
        scratch_shapes=[
            pltpu.VMEM((nq, 2 * HEAD_DIM, tq), BF16),
            pltpu.VMEM((seq, 2 * HEAD_DIM), BF16),
            pltpu.VMEM((tq, 2 * HEAD_DIM), BF16),
            pltpu.VMEM((tq, 1), F32),
            pltpu.VMEM((tq, 2 * HEAD_DIM), F32),
        ],
        compiler_params=pltpu.CompilerParams(
            dimension_semantics=("parallel", "parallel", "arbitrary"),
            vmem_limit_bytes=_vmem_limit(blocks, scratch, temps)),
        name="fox",
    )(pa, pa, pa, c)


def _proj_out_kernel(hm_ref, hp_ref, ha_ref, w_ref, x_ref, g_ref, o_ref):
    tm = x_ref.shape[0]
    y = jnp.dot(hm_ref[...], w_ref[0:MLSTM_WIDTH, :], preferred_element_type=F32)
    y = y + jnp.dot(hp_ref[...], w_ref[MLSTM_WIDTH:MLSTM_WIDTH + POOL_WIDTH, :],
                    preferred_element_type=F32)
    y = y + jnp.dot(ha_ref[...], w_ref[MLSTM_WIDTH + POOL_WIDTH:, :], preferred_element_type=F32)
    o_ref[...] = y
    _postnorm_residual(x_ref, o_ref, g_ref, o_ref, 1.0)


def _proj_out(hm, hp, ha, w_out, x, g, *, tm):
    t, d = x.shape
    assert t % tm == 0
    blocks = (_nbytes((tm, d), BF16) + _nbytes((d, d), BF16) + 2 * _nbytes((tm, d), F32))
    temps = 2 * _nbytes((tm, d), F32)
    return pl.pallas_call(
        _proj_out_kernel,
        grid=(t // tm,),
        in_specs=[
            pl.BlockSpec((tm, MLSTM_WIDTH), lambda i: (i, 0)),
            pl.BlockSpec((tm, POOL_WIDTH), lambda i: (i, 0)),
            pl.BlockSpec((tm, FOX_WIDTH), lambda i: (i, 0)),
            pl.BlockSpec((d, d), lambda i: (0, 0)),
            pl.BlockSpec((tm, d), lambda i: (i, 0)),
            pl.BlockSpec((1, d), lambda i: (0, 0)),
        ],
        out_specs=pl.BlockSpec((tm, d), lambda i: (i, 0)),
        out_shape=jax.ShapeDtypeStruct((t, d), F32),
        compiler_params=pltpu.CompilerParams(
            dimension_semantics=("parallel",),
            vmem_limit_bytes=_vmem_limit(blocks, 0, temps)),
        name="proj_out",
    )(hm, hp, ha, w_out, x, g.reshape(1, d))


def _mixer(x, pre_g, post_g, w_in_parts, conv_w, b_i, b_f, head_g, pool_w, pool_scale, fox_b_f,
           w_out, *, batch, seq):
    w_pm, w_pa, w_gate = w_in_parts
    bias = jnp.concatenate([b_i, b_f, fox_b_f]).reshape(N_GATES, 1)

    pm, pa, pg = _proj_in(x, pre_g, w_pm, w_pa, w_gate, tm=512)
    grow, gcol = _gates(pg, bias, batch=batch, seq=seq)
    hm = _mlstm(pm, gcol, grow, conv_w, head_g, batch=batch, seq=seq, tb=min(512, seq))
    hp = _pool(pm, pool_w.astype(BF16), pool_scale, batch=batch, seq=seq, ts=min(512, seq))
    ha = _fox(pa, grow, batch=batch, seq=seq, tq=min(2048, seq), n_parts=4, n_diag_parts=8)
    return _proj_out(hm, hp, ha, w_out, x, post_g, tm=512)


def kernel(x, ffn1_pre_g, ffn1_post_g, ffn1_w_gate, ffn1_w_up, ffn1_w_down, mix_pre_g, mix_post_g,
           w_in, mlstm_conv, mlstm_b_i, mlstm_b_f, mlstm_head_g, pool_w, pool_scale, fox_b_f, w_out,
           ffn2_pre_g, ffn2_post_g, ffn2_w_gate, ffn2_w_up, ffn2_w_down):
    batch, seq, d = x.shape
    depth = w_in.shape[0]
    xt = x.reshape(batch * seq, d)
    for l in range(depth):
        xt = _ffn(xt, ffn1_pre_g[l], ffn1_post_g[l], _layer_bf16(ffn1_w_gate, l),
                  _layer_bf16(ffn1_w_up, l), _layer_bf16(ffn1_w_down, l), tm=1024, tf=512)
        xt = _mixer(xt, mix_pre_g[l], mix_post_g[l], _split_w_in(w_in, l, tr=256), mlstm_conv[l],
                    mlstm_b_i[l],
                    mlstm_b_f[l], mlstm_head_g[l], pool_w[l], pool_scale[l], fox_b_f[l],
                    _layer_bf16(w_out, l), batch=batch, seq=seq)
        xt = _ffn(xt, ffn2_pre_g[l], ffn2_post_g[l], _layer_bf16(ffn2_w_gate, l),
                  _layer_bf16(ffn2_w_up, l), _layer_bf16(ffn2_w_down, l), tm=1024, tf=512)
    return xt.reshape(batch, seq, d)
```

```python
import functools

import jax
import jax.numpy as jnp
from jax import lax
from jax.experimental import pallas as pl
from jax.experimental.pallas import tpu as pltpu

F32 = jnp.float32
BF16 = jnp.bfloat16

D_MODEL = 2048
HEAD_DIM = 128
MLSTM_WIDTH = D_MODEL // 4
MLSTM_HEADS = MLSTM_WIDTH // HEAD_DIM
POOL_WIDTH = D_MODEL // 4
POOL_WINDOWS = (2, 4, 8, 16)
POOL_GROUP_DIM = POOL_WIDTH // len(POOL_WINDOWS)
FOX_WIDTH = D_MODEL - MLSTM_WIDTH - POOL_WIDTH
FOX_HEADS = FOX_WIDTH // HEAD_DIM
CONV_WIDTH = 4
MLSTM_CHUNK = 128
RMS_EPS = 1e-6
FFN_RESIDUAL_WEIGHT = 0.5

OFF_MI = 4 * MLSTM_WIDTH
OFF_MF = OFF_MI + MLSTM_HEADS
OFF_POOL = OFF_MF + MLSTM_HEADS
OFF_AQ = OFF_POOL + POOL_WIDTH
OFF_AF = OFF_AQ + 3 * FOX_WIDTH
N_IN = OFF_AF + FOX_HEADS

PM_WIDTH = 4 * MLSTM_WIDTH + POOL_WIDTH
PA_WIDTH = 3 * FOX_WIDTH
N_GATES = 2 * MLSTM_HEADS + FOX_HEADS
LOG2E = 1.4426950408889634
FOX_Q_SCALE = LOG2E * HEAD_DIM ** -0.5
LANES = 128
SUBLANES = 8
VMEM_LIMIT_CAP = 58 * 1024 * 1024


def _vmem_limit(block_bytes, scratch_bytes, temp_bytes):
    return int(min(2 * block_bytes + scratch_bytes + temp_bytes + (4 << 20), VMEM_LIMIT_CAP))


def _nbytes(shape, dtype):
    n = 1
    for s in shape:
        n *= s
    return n * jnp.dtype(dtype).itemsize


def _rms_scale(x):
    return lax.rsqrt(jnp.mean(x * x, axis=-1, keepdims=True) + RMS_EPS)


CAST_BLOCK_BYTES = 6 << 20


def _cast_kernel(w_ref, o_ref):
    o_ref[...] = w_ref[...].astype(BF16)


def _layer_bf16(w, layer):
    _, r, c = w.shape
    tr = r
    while _nbytes((tr, c), F32) > CAST_BLOCK_BYTES and tr % 2 == 0 and (tr // 2) % 16 == 0:
        tr //= 2
    assert r % tr == 0
    blocks = _nbytes((tr, c), F32) + _nbytes((tr, c), BF16)
    return pl.pallas_call(
        _cast_kernel,
        grid=(r // tr,),
        in_specs=[pl.BlockSpec((None, tr, c), lambda i: (layer, i, 0))],
        out_specs=pl.BlockSpec((tr, c), lambda i: (i, 0)),
        out_shape=jax.ShapeDtypeStruct((r, c), BF16),
        compiler_params=pltpu.CompilerParams(
            dimension_semantics=("parallel",),
            vmem_limit_bytes=_vmem_limit(blocks, 0, 0)),
        name="cast",
    )(w)


def _split_w_in_kernel(wt_ref, pm_ref, pa_ref, gate_ref):
    tr = wt_ref.shape[1]
    pm_ref[:, 0:OFF_MI] = wt_ref[0:OFF_MI, :].T.astype(BF16)
    pm_ref[:, OFF_MI:PM_WIDTH] = wt_ref[OFF_POOL:OFF_AQ, :].T.astype(BF16)
    pa_ref[...] = wt_ref[OFF_AQ:OFF_AF, :].T.astype(BF16)
    gates = jnp.concatenate([wt_ref[OFF_MI:OFF_POOL, :], wt_ref[OFF_AF:N_IN, :],
                             jnp.zeros((LANES - N_GATES, tr), F32)], axis=0)
    gate_ref[...] = gates.T.astype(BF16)


def _split_w_in(w_in, layer, *, tr):
    _, r, c = w_in.shape
    assert r % tr == 0 and c == N_IN and tr % LANES == 0
    assert all(off % SUBLANES == 0 for off in (OFF_MI, OFF_POOL, OFF_AQ, OFF_AF))
    blocks = _nbytes((c, tr), F32) + _nbytes((tr, PM_WIDTH + PA_WIDTH + LANES), BF16)
    return pl.pallas_call(
        _split_w_in_kernel,
        grid=(r // tr,),
        in_specs=[pl.BlockSpec((None, c, tr), lambda i: (layer, 0, i))],
        out_specs=[
            pl.BlockSpec((tr, PM_WIDTH), lambda i: (i, 0)),
            pl.BlockSpec((tr, PA_WIDTH), lambda i: (i, 0)),
            pl.BlockSpec((tr, LANES), lambda i: (i, 0)),
        ],
        out_shape=[
            jax.ShapeDtypeStruct((r, PM_WIDTH), BF16),
            jax.ShapeDtypeStruct((r, PA_WIDTH), BF16),
            jax.ShapeDtypeStruct((r, LANES), BF16),
        ],
        compiler_params=pltpu.CompilerParams(
            dimension_semantics=("parallel",),
            vmem_limit_bytes=_vmem_limit(blocks, 0, 2 * _nbytes((tr, c), F32))),
        name="split_w_in",
    )(jnp.swapaxes(w_in, 1, 2))


NORM_ROWS = 32
NORM_COLS = 512


def _row_rms_scale(ref, rows):
    d = ref.shape[1]
    ss = None
    for c0 in range(0, d, NORM_COLS):
        v = ref[rows, c0:c0 + NORM_COLS]
        ss = v * v if ss is None else ss + v * v
    return lax.rsqrt(jnp.sum(ss, axis=-1, keepdims=True) * (1.0 / d) + RMS_EPS)


def _normalise_rows(n_rows, src_ref, apply_fn):
    groups = 4
    trip_rows = groups * NORM_ROWS
    n_trips = n_rows // trip_rows

    def rows_of(r):
        base = pl.multiple_of(r * trip_rows, trip_rows)
        return [pl.ds(base + k * NORM_ROWS, NORM_ROWS) for k in range(groups)]

    def scales_of(r):
        return tuple(_row_rms_scale(src_ref, rows) for rows in rows_of(r))

    def body(r, scales):
        nxt = scales_of(jnp.minimum(r + 1, n_trips - 1))
        for rows, scale in zip(rows_of(r), scales):
            apply_fn(rows, scale)
        return nxt

    lax.fori_loop(0, n_trips, body, scales_of(0))


def _prenorm(x_ref, g_ref, h_ref):
    def apply(rows, scale):
        for c0 in range(0, x_ref.shape[1], NORM_COLS):
            cols = slice(c0, c0 + NORM_COLS)
            h_ref[rows, cols] = (x_ref[rows, cols] * scale * g_ref[:, cols]).astype(BF16)
    _normalise_rows(x_ref.shape[0], x_ref, apply)


def _postnorm_residual(x_ref, y_ref, g_ref, o_ref, weight):
    def apply(rows, scale):
        scale = scale * weight
        for c0 in range(0, x_ref.shape[1], NORM_COLS):
            cols = slice(c0, c0 + NORM_COLS)
            o_ref[rows, cols] = x_ref[rows, cols] + y_ref[rows, cols] * scale * g_ref[:, cols]
    _normalise_rows(x_ref.shape[0], y_ref, apply)


def _ffn_kernel(x_ref, pre_g_ref, post_g_ref, wgu_ref, wd_ref, o_ref, h_ref, *, n_f):
    j = pl.program_id(1)
    tm = x_ref.shape[0]

    @pl.when(j == 0)
    def _():
        _prenorm(x_ref, pre_g_ref, h_ref)
        o_ref[...] = jnp.zeros(o_ref.shape, F32)

    h = h_ref[...]
    tf = wd_ref.shape[0]
    gu = jnp.dot(h, wgu_ref[...], preferred_element_type=F32)
    g = gu[:, 0:tf]
    u = gu[:, tf:2 * tf]
    a = (g * jax.nn.sigmoid(g) * u).astype(BF16)
    o_ref[...] += jnp.dot(a, wd_ref[...], preferred_element_type=F32)

    @pl.when(j == n_f - 1)
    def _():
        _postnorm_residual(x_ref, o_ref, post_g_ref, o_ref, FFN_RESIDUAL_WEIGHT)


def _gate_up_kernel(wg_ref, wu_ref, o_ref, *, tf):
    for j in range(wg_ref.shape[1] // tf):
        o_ref[:, 2 * j * tf:(2 * j + 1) * tf] = wg_ref[:, j * tf:(j + 1) * tf].astype(BF16)
        o_ref[:, (2 * j + 1) * tf:(2 * j + 2) * tf] = wu_ref[:, j * tf:(j + 1) * tf].astype(BF16)


def _gate_up_bf16(wg, wu, layer, *, tf, tr):
    _, r, c = wg.shape
    assert wu.shape == wg.shape and r % tr == 0 and c % tf == 0
    blocks = 2 * _nbytes((tr, c), F32) + _nbytes((tr, 2 * c), BF16)
    return pl.pallas_call(
        functools.partial(_gate_up_kernel, tf=tf),
        grid=(r // tr,),
        in_specs=[pl.BlockSpec((None, tr, c), lambda i: (layer, i, 0)),
                  pl.BlockSpec((None, tr, c), lambda i: (layer, i, 0))],
        out_specs=pl.BlockSpec((tr, 2 * c), lambda i: (i, 0)),
        out_shape=jax.ShapeDtypeStruct((r, 2 * c), BF16),
        compiler_params=pltpu.CompilerParams(
            dimension_semantics=("parallel",),
            vmem_limit_bytes=_vmem_limit(blocks, 0, 0)),
        name="cast_gate_up",
    )(wg, wu)


def _ffn(x, pre_g, post_g, wgu, wd, *, tm, tf):
    t, d = x.shape
    f = wd.shape[0]
    assert t % tm == 0 and f % tf == 0 and FFN_RESIDUAL_WEIGHT == 0.5 and wgu.shape == (d, 2 * f)
    blocks = (_nbytes((tm, d), F32) * 2 + 2 * _nbytes((d, tf), BF16) + _nbytes((tf, d), BF16))
    scratch = _nbytes((tm, d), BF16)
    temps = 3 * _nbytes((tm, tf), F32)
    return pl.pallas_call(
        functools.partial(_ffn_kernel, n_f=f // tf),
        grid=(t // tm, f // tf),
        in_specs=[
            pl.BlockSpec((tm, d), lambda i, j: (i, 0)),
            pl.BlockSpec((1, d), lambda i, j: (0, 0)),
            pl.BlockSpec((1, d), lambda i, j: (0, 0)),
            pl.BlockSpec((d, 2 * tf), lambda i, j: (0, j)),
            pl.BlockSpec((tf, d), lambda i, j: (j, 0)),
        ],
        out_specs=pl.BlockSpec((tm, d), lambda i, j: (i, 0)),
        out_shape=jax.ShapeDtypeStruct((t, d), F32),
        scratch_shapes=[pltpu.VMEM((tm, d), BF16)],
        compiler_params=pltpu.CompilerParams(
            dimension_semantics=("parallel", "arbitrary"),
            vmem_limit_bytes=_vmem_limit(blocks, scratch, temps)),
        name="ffn",
    )(x, pre_g.reshape(1, d), post_g.reshape(1, d), wgu, wd)


def _proj_in_kernel(x_ref, g_ref, wpm_ref, wpa_ref, wgate_ref, pm_ref, pa_ref, pg_ref):
    x = x_ref[...]
    h = (x * _rms_scale(x) * g_ref[...]).astype(BF16)
    pm_ref[...] = jnp.dot(h, wpm_ref[...], preferred_element_type=F32)
    pa = jnp.dot(h, wpa_ref[...], preferred_element_type=F32)
    pa_ref[:, 0:FOX_WIDTH] = (pa[:, 0:FOX_WIDTH] * FOX_Q_SCALE).astype(BF16)
    pa_ref[:, FOX_WIDTH:] = pa[:, FOX_WIDTH:].astype(BF16)
    pg_ref[...] = jnp.dot(h, wgate_ref[...], preferred_element_type=F32)


def _proj_in(x, g, w_pm, w_pa, w_gate, *, tm):
    t, d = x.shape
    assert t % tm == 0 and w_pm.shape == (d, PM_WIDTH) and w_pa.shape == (d, PA_WIDTH)
    resident = pl.Buffered(1)
    blocks = (_nbytes((tm, d), F32) + _nbytes((d, LANES), BF16) + _nbytes((tm, PM_WIDTH), F32)
              + _nbytes((tm, PA_WIDTH), BF16) + _nbytes((tm, LANES), F32))
    scratch = _nbytes((d, PM_WIDTH + PA_WIDTH), BF16)
    temps = _nbytes((tm, d), BF16) + 2 * _nbytes((tm, PA_WIDTH), F32)
    return pl.pallas_call(
        _proj_in_kernel,
        grid=(t // tm,),
        in_specs=[
            pl.BlockSpec((tm, d), lambda i: (i, 0)),
            pl.BlockSpec((1, d), lambda i: (0, 0)),
            pl.BlockSpec((d, PM_WIDTH), lambda i: (0, 0), pipeline_mode=resident),
            pl.BlockSpec((d, PA_WIDTH), lambda i: (0, 0), pipeline_mode=resident),
            pl.BlockSpec((d, LANES), lambda i: (0, 0)),
        ],
        out_specs=[
            pl.BlockSpec((tm, PM_WIDTH), lambda i: (i, 0)),
            pl.BlockSpec((tm, PA_WIDTH), lambda i: (i, 0)),
            pl.BlockSpec((tm, LANES), lambda i: (i, 0)),
        ],
        out_shape=[
            jax.ShapeDtypeStruct((t, PM_WIDTH), F32),
            jax.ShapeDtypeStruct((t, PA_WIDTH), BF16),
            jax.ShapeDtypeStruct((t, LANES), F32),
        ],
        compiler_params=pltpu.CompilerParams(
            dimension_semantics=("parallel",),
            vmem_limit_bytes=_vmem_limit(blocks, scratch, temps)),
        name="proj_in",
    )(x, g.reshape(1, d), w_pm, w_pa, w_gate)


def _log_sigmoid(z):
    return -(jnp.maximum(-z, 0.0) + jnp.log1p(jnp.exp(-jnp.abs(z))))


def _lane_scan(v, shifts, lane_pos):
    for sh in shifts:
        v = v + jnp.where(lane_pos >= sh, pltpu.roll(v, sh, axis=1), 0.0)
    return v


def _gates_kernel(pg_ref, bias_ref, row_ref, col_ref):
    s = pg_ref.shape[0]
    z = pg_ref[...].T[0:N_GATES, :] + bias_ref[...]
    lane = lax.broadcasted_iota(jnp.int32, (1, s), 1)
    mi = z[0:MLSTM_HEADS]
    lf_m = _log_sigmoid(z[MLSTM_HEADS:2 * MLSTM_HEADS])
    lf_a = _log_sigmoid(z[2 * MLSTM_HEADS:N_GATES])
    chunk_shifts = [1 << k for k in range(MLSTM_CHUNK.bit_length() - 1)]
    seq_shifts = [1 << k for k in range((s - 1).bit_length())]
    bcum = _lane_scan(lf_m, chunk_shifts, lane % MLSTM_CHUNK)
    cfox = _lane_scan(lf_a, seq_shifts, lane)
    rows = jnp.concatenate([mi, bcum, cfox], axis=0)
    row_ref[0] = rows
    padded = jnp.concatenate([rows, jnp.zeros((LANES - N_GATES, s), F32)], axis=0)
    col_ref[...] = padded.T


def _gates(pg, bias, *, batch, seq):
    t = pg.shape[0]
    assert t == batch * seq
    blocks = 2 * _nbytes((seq, LANES), F32) + _nbytes((N_GATES, seq), F32)
    temps = 6 * _nbytes((seq, LANES), F32)
    return pl.pallas_call(
        _gates_kernel,
        grid=(batch,),
        in_specs=[
            pl.BlockSpec((seq, LANES), lambda b: (b, 0)),
            pl.BlockSpec((N_GATES, 1), lambda b: (0, 0)),
        ],
        out_specs=[
            pl.BlockSpec((1, N_GATES, seq), lambda b: (b, 0, 0)),
            pl.BlockSpec((seq, LANES), lambda b: (b, 0)),
        ],
        out_shape=[
            jax.ShapeDtypeStruct((batch, N_GATES, seq), F32),
            jax.ShapeDtypeStruct((t, LANES), F32),
        ],
        compiler_params=pltpu.CompilerParams(
            dimension_semantics=("parallel",),
            vmem_limit_bytes=_vmem_limit(blocks, 0, temps)),
        name="gates",
    )(pg, bias)


def _mlstm_kernel(qk_ref, v_ref, o_ref, gcol_ref, grow_ref, convw_ref, hg_ref, out_ref,
                  ubuf, qkc, cn_ref, m_ref, *, tb):
    step = pl.program_id(1)
    L = MLSTM_CHUNK
    halo = SUBLANES

    @pl.when(step == 0)
    def _():
        ubuf[0:halo, :] = jnp.zeros((halo, 2 * MLSTM_WIDTH), F32)
        cn_ref[...] = jnp.zeros(cn_ref.shape, F32)
        m_ref[...] = jnp.zeros(m_ref.shape, F32)

    @pl.when(step > 0)
    def _():
        ubuf[0:halo, :] = ubuf[tb:tb + halo, :]

    ubuf[halo:halo + tb, :] = qk_ref[...]
    w = convw_ref[...]
    y = ubuf[halo:halo + tb, :] * w[0:1, :]
    for j in range(1, CONV_WIDTH):
        y = y + ubuf[halo - j:halo - j + tb, :] * w[j:j + 1, :]
    qkc[...] = y * jax.nn.sigmoid(y)

    tri = (lax.broadcasted_iota(jnp.int32, (L, L), 0) >= lax.broadcasted_iota(jnp.int32, (L, L), 1))
    ones_col = (lax.broadcasted_iota(jnp.int32, (L, HEAD_DIM), 1) == 0).astype(F32)
    q_scale = HEAD_DIM ** -0.5

    for c in range(tb // L):
        rows = slice(c * L, (c + 1) * L)
        for hd in range(MLSTM_HEADS):
            hs = slice(hd * HEAD_DIM, (hd + 1) * HEAD_DIM)
            ks = slice(MLSTM_WIDTH + hd * HEAD_DIM, MLSTM_WIDTH + (hd + 1) * HEAD_DIM)
            q = (qkc[rows, hs] * q_scale).astype(BF16)
            k = qkc[rows, ks]
            k_bf = k.astype(BF16)
            vaug = jnp.concatenate([v_ref[rows, hs], ones_col], axis=1)
            i_col = gcol_ref[rows, hd:hd + 1]
            b_col = gcol_ref[rows, MLSTM_HEADS + hd:MLSTM_HEADS + hd + 1]
            i_row = grow_ref[0, hd:hd + 1, rows]
            b_row = grow_ref[0, MLSTM_HEADS + hd:MLSTM_HEADS + hd + 1, rows]
            m_prev = m_ref[hd][0:1, 0:1]
            cn = cn_ref[hd]

            log_d = jnp.where(tri, b_col - b_row + i_row, -jnp.inf)
            inter = b_col + m_prev
            m_t = jnp.maximum(inter, jnp.max(log_d, axis=-1, keepdims=True))
            s = lax.dot_general(q, k_bf, (((1,), (1,)), ((), ())), preferred_element_type=F32)
            scores = (s * jnp.exp(log_d - m_t)).astype(BF16)
            inter_w = jnp.exp(inter - m_t)
            numden = (jnp.dot(scores, vaug.astype(BF16), preferred_element_type=F32)
                      + inter_w * jnp.dot(q, cn.astype(BF16), preferred_element_type=F32))
            num = numden[:, 0:HEAD_DIM]
            den = numden[:, HEAD_DIM:HEAD_DIM + 1]
            h = num / jnp.maximum(jnp.abs(den), jnp.exp(-m_t))

            b_last = b_row[:, L - 1:L]
            log_w = b_last - b_col + i_col
            m_new = jnp.maximum(b_last + m_prev, jnp.max(log_w, axis=0, keepdims=True))
            wv = (jnp.exp(log_w - m_new) * vaug).astype(BF16)
            decay = jnp.exp(b_last + m_prev - m_new)
            cn_ref[hd] = decay * cn + jnp.dot(k.T.astype(BF16), wv, preferred_element_type=F32)
            m_ref[hd] = jnp.broadcast_to(m_new, (SUBLANES, LANES))

            hn = h * _rms_scale(h)
            out_ref[rows, hs] = (hn * hg_ref[:, hs] * jax.nn.sigmoid(o_ref[rows, hs])).astype(BF16)


def _mlstm(pm, gcol, grow, conv_w, head_g, *, batch, seq, tb):
    t = pm.shape[0]
    assert seq % tb == 0 and tb % MLSTM_CHUNK == 0
    nb = seq // tb
    w2 = 2 * MLSTM_WIDTH
    blocks = (_nbytes((tb, w2), F32) + 2 * _nbytes((tb, MLSTM_WIDTH), F32) + _nbytes((tb, LANES), F32)
              + _nbytes((N_GATES, tb), F32) + _nbytes((tb, MLSTM_WIDTH), BF16))
    scratch = (_nbytes((tb + 2 * SUBLANES, w2), F32) + _nbytes((tb, w2), F32)
               + _nbytes((MLSTM_HEADS, HEAD_DIM, 2 * HEAD_DIM), F32))
    temps = 2 * _nbytes((tb, w2), F32)
    return pl.pallas_call(
        functools.partial(_mlstm_kernel, tb=tb),
        grid=(batch, nb),
        in_specs=[
            pl.BlockSpec((tb, w2), lambda b, i: (b * nb + i, 0)),
            pl.BlockSpec((tb, MLSTM_WIDTH), lambda b, i: (b * nb + i, 2)),
            pl.BlockSpec((tb, MLSTM_WIDTH), lambda b, i: (b * nb + i, 3)),
            pl.BlockSpec((tb, LANES), lambda b, i: (b * nb + i, 0)),
            pl.BlockSpec((1, N_GATES, tb), lambda b, i: (b, 0, i)),
            pl.BlockSpec((CONV_WIDTH, w2), lambda b, i: (0, 0)),
            pl.BlockSpec((1, MLSTM_WIDTH), lambda b, i: (0, 0)),
        ],
        out_specs=pl.BlockSpec((tb, MLSTM_WIDTH), lambda b, i: (b * nb + i, 0)),
        out_shape=jax.ShapeDtypeStruct((t, MLSTM_WIDTH), BF16),
        scratch_shapes=[
            pltpu.VMEM((tb + 2 * SUBLANES, w2), F32),
            pltpu.VMEM((tb, w2), F32),
            pltpu.VMEM((MLSTM_HEADS, HEAD_DIM, 2 * HEAD_DIM), F32),
            pltpu.VMEM((MLSTM_HEADS, SUBLANES, LANES), F32),
        ],
        compiler_params=pltpu.CompilerParams(
            dimension_semantics=("parallel", "arbitrary"),
            vmem_limit_bytes=_vmem_limit(blocks, scratch, temps)),
        name="mlstm",
    )(pm, pm, pm, gcol, grow, conv_w, head_g.reshape(1, MLSTM_WIDTH))


def _pool_kernel(u_ref, w_ref, scale_ref, out_ref, ext, *, ts):
    step = pl.program_id(1)
    halo = 2 * SUBLANES
    assert halo >= max(POOL_WINDOWS)

    @pl.when(step == 0)
    def _():
        ext[0:halo, :] = jnp.zeros((halo, POOL_WIDTH), F32)

    @pl.when(step > 0)
    def _():
        ext[0:halo, :] = ext[ts:ts + halo, :]

    ext[halo:halo + ts, :] = u_ref[...]
    pos = (step * ts + lax.broadcasted_iota(jnp.int32, (ts, 1), 0) + 1).astype(F32)
    for g, win in enumerate(POOL_WINDOWS):
        cols = slice(g * POOL_GROUP_DIM, (g + 1) * POOL_GROUP_DIM)
        acc = ext[halo:halo + ts, cols]
        for j in range(1, win):
            acc = acc + ext[halo - j:halo - j + ts, cols]
        mean = acc / jnp.minimum(pos, float(win))
        diff = (mean - ext[halo:halo + ts, cols]).astype(BF16)
        y = jnp.dot(diff, w_ref[g], preferred_element_type=F32)
        out_ref[:, cols] = (y * scale_ref[:, cols]).astype(BF16)


def _pool(pm, pool_w, pool_scale, *, batch, seq, ts):
    t = pm.shape[0]
    assert seq % ts == 0
    nb = seq // ts
    col_block = (4 * MLSTM_WIDTH) // POOL_WIDTH
    blocks = (_nbytes((ts, POOL_WIDTH), F32) + _nbytes(pool_w.shape, BF16)
              + _nbytes((ts, POOL_WIDTH), BF16))
    scratch = _nbytes((ts + 2 * SUBLANES, POOL_WIDTH), F32)
    temps = 4 * _nbytes((ts, POOL_GROUP_DIM), F32)
    return pl.pallas_call(
        functools.partial(_pool_kernel, ts=ts),
        grid=(batch, nb),
        in_specs=[
            pl.BlockSpec((ts, POOL_WIDTH), lambda b, i: (b * nb + i, col_block)),
            pl.BlockSpec(pool_w.shape, lambda b, i: (0, 0, 0)),
            pl.BlockSpec((1, POOL_WIDTH), lambda b, i: (0, 0)),
        ],
        out_specs=pl.BlockSpec((ts, POOL_WIDTH), lambda b, i: (b * nb + i, 0)),
        out_shape=jax.ShapeDtypeStruct((t, POOL_WIDTH), BF16),
        scratch_shapes=[pltpu.VMEM((ts + 2 * SUBLANES, POOL_WIDTH), F32)],
        compiler_params=pltpu.CompilerParams(
            dimension_semantics=("parallel", "arbitrary"),
            vmem_limit_bytes=_vmem_limit(blocks, scratch, temps)),
        name="pool",
    )(pm, pool_w, pool_scale.reshape(1, POOL_WIDTH))


def _split3_bf16(c):
    hi = c.astype(BF16).astype(F32)
    r = c - hi
    mid = r.astype(BF16).astype(F32)
    lo = (r - mid).astype(BF16).astype(F32)
    return hi, mid, lo


def _bias_rows(c_row, n, *, query_side):
    hi, mid, lo = _split3_bf16(c_row)
    ridx = lax.broadcasted_iota(jnp.int32, (HEAD_DIM, n), 0)
    sign = 1.0 if query_side else -1.0
    first = 0 if query_side else 3
    ones_at = 3 if query_side else 0
    rows = jnp.where(ridx == first, sign * hi, 0.0)
    rows = jnp.where(ridx == first + 1, sign * mid, rows)
    rows = jnp.where(ridx == first + 2, sign * lo, rows)
    return jnp.where((ridx >= ones_at) & (ridx < ones_at + 3), 1.0, rows)


def _fox_kernel(q_ref, k_ref, v_ref, c_ref, out_ref, kt_ref, va_ref, qa_ref, m_ref, acc_ref,
                *, tq, nq, n_parts, n_diag_parts):
    qi = pl.program_id(2)

    @pl.when(qi == 0)
    def _():
        ones_col = (lax.broadcasted_iota(jnp.int32, (tq, HEAD_DIM), 1) == 0).astype(BF16)
        for jb in range(nq):
            rows = slice(jb * tq, (jb + 1) * tq)
            kt = k_ref[rows, :].astype(F32).T
            ext = _bias_rows(c_ref[0, jb:jb + 1, :] * LOG2E, tq, query_side=False)
            kt_ref[jb] = jnp.concatenate([kt, ext], axis=0).astype(BF16)
            va_ref[rows, 0:HEAD_DIM] = v_ref[rows, :]
            va_ref[rows, HEAD_DIM:] = ones_col

    c2_q = c_ref[0, pl.ds(qi, 1), :] * LOG2E
    qa_ref[:, 0:HEAD_DIM] = q_ref[...]
    qa_ref[:, HEAD_DIM:] = _bias_rows(c2_q, tq, query_side=True).T.astype(BF16)
    m_ref[...] = jnp.full(m_ref.shape, -jnp.inf, F32)
    acc_ref[...] = jnp.zeros(acc_ref.shape, F32)

    def logits(r0, nr, kv_blk, nc, masked):
        s = jnp.dot(qa_ref[r0:r0 + nr, :], kt_ref[kv_blk][:, 0:nc], preferred_element_type=F32)
        if masked:
            keep = (lax.broadcasted_iota(jnp.int32, (nr, nc), 1)
                    <= lax.broadcasted_iota(jnp.int32, (nr, nc), 0) + r0)
            s = jnp.where(keep, s, -jnp.inf)
        return s

    def accumulate(s, r0, nr, kv_blk, nc):
        m_prev = m_ref[r0:r0 + nr, :]
        m_new = jnp.maximum(m_prev, jnp.max(s, axis=-1, keepdims=True))
        alpha = jnp.exp2(m_prev - m_new)
        p = jnp.exp2(s - m_new).astype(BF16)
        v_rows = pl.ds(pl.multiple_of(kv_blk * tq, tq), nc)
        acc_ref[r0:r0 + nr, :] = (alpha * acc_ref[r0:r0 + nr, :]
                                  + jnp.dot(p, va_ref[v_rows, :], preferred_element_type=F32))
        m_ref[r0:r0 + nr, :] = m_new

    def update(parts, kv_blk, masked):
        ss = [logits(r0, nr, kv_blk, nc, masked) for r0, nr, nc in parts]
        for s, (r0, nr, nc) in zip(ss, parts):
            accumulate(s, r0, nr, kv_blk, nc)

    pr = tq // n_parts

    def body(j, carry):
        update([(i * pr, pr, tq) for i in range(n_parts)], j, False)
        return carry

    lax.fori_loop(0, qi, body, 0)
    dr = tq // n_diag_parts
    update([(i * dr, dr, (i + 1) * dr) for i in range(n_diag_parts)], qi, True)
    out_ref[...] = (acc_ref[:, 0:HEAD_DIM] / acc_ref[:, HEAD_DIM:HEAD_DIM + 1]).astype(BF16)


def _fox(pa, grow, *, batch, seq, tq, n_parts, n_diag_parts):
    t = pa.shape[0]
    assert seq % tq == 0
    nq = seq // tq
    assert tq % n_parts == 0 and (tq // n_parts) % LANES == 0
    assert tq % n_diag_parts == 0 and (tq // n_diag_parts) % LANES == 0
    c = grow.reshape(batch * N_GATES, nq, tq)
    gate0 = 2 * MLSTM_HEADS
    blocks = (2 * _nbytes((tq, HEAD_DIM), BF16) + 2 * _nbytes((seq, HEAD_DIM), BF16)
              + _nbytes((max(nq, SUBLANES), tq), F32))
    scratch = (2 * _nbytes((seq, 2 * HEAD_DIM), BF16) + _nbytes((tq, 2 * HEAD_DIM), BF16)
               + _nbytes((tq, LANES), F32) + _nbytes((tq, 2 * HEAD_DIM), F32))
    temps = 3 * _nbytes((tq, tq), F32)
    return pl.pallas_call(
        functools.partial(_fox_kernel, tq=tq, nq=nq, n_parts=n_parts, n_diag_parts=n_diag_parts),
        grid=(batch, FOX_HEADS, nq),
        in_specs=[
            pl.BlockSpec((tq, HEAD_DIM), lambda b, h, i: (b * nq + i, h)),
            pl.BlockSpec((seq, HEAD_DIM), lambda b, h, i: (b, FOX_HEADS + h)),
            pl.BlockSpec((seq, HEAD_DIM), lambda b, h, i: (b, 2 * FOX_HEADS + h)),
            pl.BlockSpec((1, nq, tq), lambda b, h, i: (b * N_GATES + gate0 + h, 0, 0)),
        ],
        out_specs=pl.BlockSpec((tq, HEAD_DIM), lambda b, h, i: (b * nq + i, h)),
        out_shape=jax.ShapeDtypeStruct((t, FOX_WIDTH), BF16),
        scratch_shapes=[
            pltpu.VMEM((nq, 2 * HEAD_DIM, tq), BF16),
            pltpu.VMEM((seq, 2 * HEAD_DIM), BF16),
            pltpu.VMEM((tq, 2 * HEAD_DIM), BF16),
            pltpu.VMEM((tq, 1), F32),
            pltpu.VMEM((tq, 2 * HEAD_DIM), F32),
        ],
        compiler_params=pltpu.CompilerParams(
            dimension_semantics=("parallel", "parallel", "arbitrary"),
            vmem_limit_bytes=_vmem_limit(blocks, scratch, temps)),
        name="fox",
    )(pa, pa, pa, c)


def _proj_out_kernel(hm_ref, hp_ref, ha_ref, w_ref, x_ref, g_ref, o_ref):
    tm = x_ref.shape[0]
    y = jnp.dot(hm_ref[...], w_ref[0:MLSTM_WIDTH, :], preferred_element_type=F32)
    y = y + jnp.dot(hp_ref[...], w_ref[MLSTM_WIDTH:MLSTM_WIDTH + POOL_WIDTH, :],
                    preferred_element_type=F32)
    y = y + jnp.dot(ha_ref[...], w_ref[MLSTM_WIDTH + POOL_WIDTH:, :], preferred_element_type=F32)
    o_ref[...] = y
    _postnorm_residual(x_ref, o_ref, g_ref, o_ref, 1.0)


def _proj_out(hm, hp, ha, w_out, x, g, *, tm):
    t, d = x.shape
    assert t % tm == 0
    blocks = (_nbytes((tm, d), BF16) + _nbytes((d, d), BF16) + 2 * _nbytes((tm, d), F32))
    temps = 2 * _nbytes((tm, d), F32)
    return pl.pallas_call(
        _proj_out_kernel,
        grid=(t // tm,),
        in_specs=[
            pl.BlockSpec((tm, MLSTM_WIDTH), lambda i: (i, 0)),
            pl.BlockSpec((tm, POOL_WIDTH), lambda i: (i, 0)),
            pl.BlockSpec((tm, FOX_WIDTH), lambda i: (i, 0)),
            pl.BlockSpec((d, d), lambda i: (0, 0)),
            pl.BlockSpec((tm, d), lambda i: (i, 0)),
            pl.BlockSpec((1, d), lambda i: (0, 0)),
        ],
        out_specs=pl.BlockSpec((tm, d), lambda i: (i, 0)),
        out_shape=jax.ShapeDtypeStruct((t, d), F32),
        compiler_params=pltpu.CompilerParams(
            dimension_semantics=("parallel",),
            vmem_limit_bytes=_vmem_limit(blocks, 0, temps)),
        name="proj_out",
    )(hm, hp, ha, w_out, x, g.reshape(1, d))


def _mixer(x, pre_g, post_g, w_in_parts, conv_w, b_i, b_f, head_g, pool_w, pool_scale, fox_b_f,
           w_out, *, batch, seq):
    w_pm, w_pa, w_gate = w_in_parts
    bias = jnp.concatenate([b_i, b_f, fox_b_f]).reshape(N_GATES, 1)

    pm, pa, pg = _proj_in(x, pre_g, w_pm, w_pa, w_gate, tm=512)
    grow, gcol = _gates(pg, bias, batch=batch, seq=seq)
    hm = _mlstm(pm, gcol, grow, conv_w, head_g, batch=batch, seq=seq, tb=min(512, seq))
    hp = _pool(pm, pool_w.astype(BF16), pool_scale, batch=batch, seq=seq, ts=min(512, seq))
    ha = _fox(pa, grow, batch=batch, seq=seq, tq=min(2048, seq), n_parts=4, n_diag_parts=8)
    return _proj_out(hm, hp, ha, w_out, x, post_g, tm=512)


def kernel(x, ffn1_pre_g, ffn1_post_g, ffn1_w_gate, ffn1_w_up, ffn1_w_down, mix_pre_g, mix_post_g,
           w_in, mlstm_conv, mlstm_b_i, mlstm_b_f, mlstm_head_g, pool_w, pool_scale, fox_b_f, w_out,
           ffn2_pre_g, ffn2_post_g, ffn2_w_gate, ffn2_w_up, ffn2_w_down):
    batch, seq, d = x.shape
    depth = w_in.shape[0]
    xt = x.reshape(batch * seq, d)
    for l in range(depth):
        xt = _ffn(xt, ffn1_pre_g[l], ffn1_post_g[l],
                  _gate_up_bf16(ffn1_w_gate, ffn1_w_up, l, tf=512, tr=128),
                  _layer_bf16(ffn1_w_down, l), tm=1024, tf=512)
        xt = _mixer(xt, mix_pre_g[l], mix_post_g[l], _split_w_in(w_in, l, tr=256), mlstm_conv[l],
                    mlstm_b_i[l],
                    mlstm_b_f[l], mlstm_head_g[l], pool_w[l], pool_scale[l], fox_b_f[l],
                    _layer_bf16(w_out, l), batch=batch, seq=seq)
        xt = _ffn(xt, ffn2_pre_g[l], ffn2_post_g[l],
                  _gate_up_bf16(ffn2_w_gate, ffn2_w_up, l, tf=512, tr=128),
                  _layer_bf16(ffn2_w_down, l), tm=1024, tf=512)
    return xt.reshape(batch, seq, d)
```
